```python
import jax
import jax.numpy as jnp
from jax import lax
import numpy as np

D_MODEL = 2048
BATCH = 8
SEQ = 2048
DEPTH = 2

CTX_LEN = 256
GRID_W = 64

MLA_HEADS = 8
Q_LORA_RANK = 512
KV_LORA_RANK = 512
NOPE_DIM = 128
ROPE_DIM = 64
V_DIM = 128
ROPE_BASE = 10000.0
Q_BLOCK = 128

CHUNK = 128
GMLP_GROUPS = 4
GMLP_WIDTH = 512

CONV_WIDTH = 512
CONV_K = 3

N_BRANCHES = 3

N_GROUPS = 4
EXPERTS_PER_GROUP = 8
N_EXPERTS = N_GROUPS * EXPERTS_PER_GROUP
TOP_K = 2
EXPERT_HIDDEN = D_MODEL // 2
EXPERT_BLOCK = 128

DN_ALPHA = (2 * DEPTH) ** 0.25
DN_BETA = (8 * DEPTH) ** -0.25
LN_EPS = 1e-6

OFF_KV = Q_LORA_RANK
OFF_GM = OFF_KV + KV_LORA_RANK + ROPE_DIM
OFF_CV = OFF_GM + 2 * GMLP_WIDTH
OFF_GATE = OFF_CV + 3 * CONV_WIDTH
P_IN = OFF_GATE + N_BRANCHES * D_MODEL

F32 = jnp.float32

kernel_name = 'hybrid_mla_gmlp_shortconv_hmoe_dit'


def layer_norm(x, g=None, b=None):
    xf = x.astype(F32)
    mu = jnp.mean(xf, axis=-1, keepdims=True)
    var = jnp.mean(jnp.square(xf - mu), axis=-1, keepdims=True)
    y = (xf - mu) * lax.rsqrt(var + LN_EPS)
    if g is not None:
        y = y * g.astype(F32) + b.astype(F32)
    return y.astype(x.dtype)


def rms_norm(x, g):
    xf = x.astype(F32)
    y = xf * lax.rsqrt(jnp.mean(jnp.square(xf), axis=-1, keepdims=True) + LN_EPS) * g.astype(F32)
    return y.astype(x.dtype)


def _seg(m, i):
    return m[..., i * D_MODEL:(i + 1) * D_MODEL]


def modulate(x, shift, scale):
    return layer_norm(x) * (1 + scale) + shift


def post_norm_residual(x, gate, y, g, b):
    return layer_norm(DN_ALPHA * x + gate * y, g, b)


def axial_rope_tables(n_tokens):
    n_rows = n_tokens // GRID_W
    t = jnp.arange(n_rows * GRID_W)
    row = (t // GRID_W).astype(F32)
    col = (t % GRID_W).astype(F32)
    half = ROPE_DIM // 2
    inv = ROPE_BASE ** (-jnp.arange(0, half, 2, dtype=F32) / half)
    ang_r = row[:, None] * inv
    ang_c = col[:, None] * inv
    ang = jnp.concatenate([ang_r, ang_r, ang_c, ang_c], axis=-1)
    return jnp.cos(ang), jnp.sin(ang)


def apply_axial_rope(t, cos, sin):
    tf = t.astype(F32)
    t1, t2, t3, t4 = jnp.split(tf, 4, axis=-1)
    rot = jnp.concatenate([-t2, t1, -t4, t3], axis=-1)
    return (tf * cos + rot * sin).astype(t.dtype)


def mla_queries(q_a, g_q, w_uq, rope):
    b, l, _ = q_a.shape
    q = (rms_norm(q_a, g_q) @ w_uq).reshape(b, l, MLA_HEADS, NOPE_DIM + ROPE_DIM)
    q_nope, q_rope = q[..., :NOPE_DIM], q[..., NOPE_DIM:]
    if rope is not None:
        cos, sin = rope
        q_rope = apply_axial_rope(q_rope, cos[None, :, None], sin[None, :, None])
    return q_nope, q_rope


def mla_keys_values(kv_a, g_kv, w_ukv, rope):
    b, l, _ = kv_a.shape
    c_kv = rms_norm(kv_a[..., :KV_LORA_RANK], g_kv)
    k_rope = kv_a[..., KV_LORA_RANK:]
    if rope is not None:
        cos, sin = rope
        k_rope = apply_axial_rope(k_rope, cos[None], sin[None])
    kv = (c_kv @ w_ukv).reshape(b, l, MLA_HEADS, NOPE_DIM + V_DIM)
    return kv[..., :NOPE_DIM], k_rope, kv[..., NOPE_DIM:]


def block_attention(q_nope, q_rope, k_nope, k_rope, v):
    b, lq, h, _ = q_nope.shape
    nb = lq // Q_BLOCK
    scale = (NOPE_DIM + ROPE_DIM) ** -0.5

    def to_blocks(t):
        return jnp.moveaxis(t.reshape(b, nb, Q_BLOCK, *t.shape[2:]), 1, 0)

    def one_block(qs):
        qn, qr = qs
        s = (jnp.einsum('bqhd,bkhd->bhqk', qn, k_nope).astype(F32)
             + jnp.einsum('bqhr,bkr->bhqk', qr, k_rope).astype(F32)) * scale
        p = jax.nn.softmax(s, axis=-1).astype(v.dtype)
        return jnp.einsum('bhqk,bkhd->bqhd', p, v)

    out = lax.map(one_block, (to_blocks(q_nope), to_blocks(q_rope)))
    return jnp.moveaxis(out, 0, 1).reshape(b, lq, h * V_DIM)


def chunk_gmlp(pre, ln_g, ln_b, w_s, b_s):
    b, l, _ = pre.shape
    z = jax.nn.gelu(pre)
    u, v = z[..., :GMLP_WIDTH], z[..., GMLP_WIDTH:]
    v = layer_norm(v, ln_g, ln_b).reshape(b, l // CHUNK, CHUNK, GMLP_GROUPS, GMLP_WIDTH // GMLP_GROUPS)
    v = jnp.einsum('gpq,bnqgc->bnpgc', w_s, v) + b_s.T[:, :, None]
    return u * v.reshape(b, l, GMLP_WIDTH)


def short_gated_conv(pre, w_conv):
    l = pre.shape[1]
    gate_b = pre[..., :CONV_WIDTH]
    gate_c = pre[..., CONV_WIDTH:2 * CONV_WIDTH]
    x_in = pre[..., 2 * CONV_WIDTH:]
    y = gate_c * x_in
    pad = CONV_K // 2
    yp = jnp.pad(y, ((0, 0), (pad, pad), (0, 0)))
    conv = yp[:, 0:l] * w_conv[0]
    for k in range(1, CONV_K):
        conv = conv + yp[:, k:k + l] * w_conv[k]
    return gate_b * conv


def merge_branches(p, attn, ln_g, ln_b, w_s, b_s, w_conv, w_oa, w_ob, w_oc, w_o):
    gm = chunk_gmlp(p[..., OFF_GM:OFF_CV], ln_g, ln_b, w_s, b_s)
    cv = short_gated_conv(p[..., OFF_CV:OFF_GATE], w_conv)
    g = jax.nn.sigmoid(p[..., OFF_GATE:].astype(F32)).astype(p.dtype)
    merged = (_seg(g, 0) * (attn @ w_oa)
              + _seg(g, 1) * (gm @ w_ob)
              + _seg(g, 2) * (cv @ w_oc))
    return merged @ w_o


def grouped_expert_ffn(t, expert_ids, weights, w1, w3, w2):
    n_tok, d = t.shape
    k = expert_ids.shape[1]
    a = n_tok * k
    flat_e = expert_ids.reshape(a)
    order = jnp.argsort(flat_e)
    sorted_e = flat_e[order]
    counts = jnp.zeros((N_EXPERTS,), jnp.int32).at[flat_e].add(1)
    starts = jnp.cumsum(counts) - counts
    padded = (counts + EXPERT_BLOCK - 1) // EXPERT_BLOCK * EXPERT_BLOCK
    pend = jnp.cumsum(padded)
    pstarts = pend - padded
    rank = jnp.arange(a, dtype=jnp.int32) - starts[sorted_e]
    dest = jnp.zeros((a,), jnp.int32).at[order].set(pstarts[sorted_e] + rank)
    n_blocks = -(-a // EXPERT_BLOCK) + N_EXPERTS
    tok = jnp.arange(a) // k
    buf = jnp.zeros((n_blocks * EXPERT_BLOCK, d), t.dtype).at[dest].set(t[tok])
    block_e = jnp.clip(jnp.searchsorted(pend, jnp.arange(n_blocks, dtype=jnp.int32) * EXPERT_BLOCK,
                                        side='right'), 0, N_EXPERTS - 1)

    def run_block(args):
        xb, e = args
        hb = jax.nn.silu(xb @ w1[e]) * (xb @ w3[e])
        return hb @ w2[e]

    ybuf = lax.map(run_block, (buf.reshape(n_blocks, EXPERT_BLOCK, d), block_e))
    y = ybuf.reshape(n_blocks * EXPERT_BLOCK, d)[dest].reshape(n_tok, k, d)
    return jnp.einsum('tk,tkd->td', weights.astype(y.dtype), y)


def hier_moe(t, w_group, b_group, w_expert, b_expert, w1, w3, w2):
    n_tok = t.shape[0]
    g_prob = jax.nn.softmax((t @ w_group).astype(F32) + b_group.astype(F32), axis=-1)
    g_w, g_idx = lax.top_k(g_prob, 1)
    e_logits = ((t @ w_expert).astype(F32) + b_expert.astype(F32)).reshape(n_tok, N_GROUPS, EXPERTS_PER_GROUP)
    e_logits = e_logits[jnp.arange(n_tok), g_idx[:, 0]]
    e_w, e_idx = lax.top_k(jax.nn.softmax(e_logits, axis=-1), TOP_K)
    weights = g_w * e_w / jnp.sum(e_w, axis=-1, keepdims=True)
    expert_ids = g_idx * EXPERTS_PER_GROUP + e_idx
    return grouped_expert_ffn(t, expert_ids, weights, w1, w3, w2)


def setup_inputs(seed: int = 0) -> dict:
    key = jax.random.key(seed)
    ks = iter(jax.random.split(key, 40))

    def nrm(shape, scale):
        return jax.random.normal(next(ks), shape, jnp.float32) * scale

    d, nl, h = D_MODEL, DEPTH, MLA_HEADS
    return {
        'x': nrm((BATCH, SEQ, d), 1.0),
        'c': nrm((BATCH, d), 1.0),
        'ctx': nrm((BATCH, CTX_LEN, d), 1.0),
        'c_ctx': nrm((d,), 1.0),
        'w_mod': nrm((nl, d, 6 * d), 0.5 * d ** -0.5),
        'b_mod': nrm((nl, 6 * d), 0.02),
        'w_in': nrm((nl, d, P_IN), d ** -0.5),
        'g_q': 1.0 + nrm((nl, Q_LORA_RANK), 0.02),
        'w_uq': nrm((nl, Q_LORA_RANK, h * (NOPE_DIM + ROPE_DIM)), Q_LORA_RANK ** -0.5),
        'g_kv': 1.0 + nrm((nl, KV_LORA_RANK), 0.02),
        'w_ukv': nrm((nl, KV_LORA_RANK, h * (NOPE_DIM + V_DIM)), KV_LORA_RANK ** -0.5),
        'w_oa': nrm((nl, h * V_DIM, d), (h * V_DIM) ** -0.5),
        'gm_ln_g': 1.0 + nrm((nl, GMLP_WIDTH), 0.02),
        'gm_ln_b': nrm((nl, GMLP_WIDTH), 0.02),
        'w_s': nrm((nl, GMLP_GROUPS, CHUNK, CHUNK), 0.5 * CHUNK ** -0.5),
        'b_s': 1.0 + nrm((nl, GMLP_GROUPS, CHUNK), 0.02),
        'w_ob': nrm((nl, GMLP_WIDTH, d), GMLP_WIDTH ** -0.5),
        'w_conv': nrm((nl, CONV_K, CONV_WIDTH), CONV_K ** -0.5),
        'w_oc': nrm((nl, CONV_WIDTH, d), CONV_WIDTH ** -0.5),
        'w_o': nrm((nl, d, d), DN_BETA * d ** -0.5),
        'ln1_g': 1.0 + nrm((nl, d), 0.02),
        'ln1_b': nrm((nl, d), 0.02),
        'w_group': nrm((nl, d, N_GROUPS), d ** -0.5),
        'b_group': nrm((nl, N_GROUPS), 0.01),
        'w_expert': nrm((nl, d, N_EXPERTS), d ** -0.5),
        'b_expert': nrm((nl, N_EXPERTS), 0.01),
        'w1': nrm((nl, N_EXPERTS, d, EXPERT_HIDDEN), d ** -0.5),
        'w3': nrm((nl, N_EXPERTS, d, EXPERT_HIDDEN), d ** -0.5),
        'w2': nrm((nl, N_EXPERTS, EXPERT_HIDDEN, d), DN_BETA * EXPERT_HIDDEN ** -0.5),
        'ln2_g': 1.0 + nrm((nl, d), 0.02),
        'ln2_b': nrm((nl, d), 0.02),
    }


def reference(x, c, ctx, c_ctx, w_mod, b_mod, w_in, g_q, w_uq, g_kv, w_ukv, w_oa, gm_ln_g, gm_ln_b,
              w_s, b_s, w_ob, w_conv, w_oc, w_o, ln1_g, ln1_b, w_group, b_group, w_expert, b_expert,
              w1, w3, w2, ln2_g, ln2_b):
    bsz, seq, d = x.shape
    rope = axial_rope_tables(seq)
    xc = ctx
    for l in range(DEPTH):
        last = l == DEPTH - 1
        m = (jax.nn.silu(c) @ w_mod[l] + b_mod[l])[:, None, :]
        mc = jax.nn.silu(c_ctx) @ w_mod[l] + b_mod[l]

        h = modulate(x, _seg(m, 0), _seg(m, 1))
        hc = modulate(xc, _seg(mc, 0), _seg(mc, 1))
        p = h @ w_in[l]
        if last:
            pc_kv = hc @ w_in[l][:, OFF_KV:OFF_GM]
        else:
            pc = hc @ w_in[l]
            pc_kv = pc[..., OFF_KV:OFF_GM]
        kn_c, kr_c, v_c = mla_keys_values(pc_kv, g_kv[l], w_ukv[l], None)
        kn_x, kr_x, v_x = mla_keys_values(p[..., OFF_KV:OFF_GM], g_kv[l], w_ukv[l], rope)
        qn, qr = mla_queries(p[..., :OFF_KV], g_q[l], w_uq[l], rope)
        attn = block_attention(qn, qr,
                               jnp.concatenate([kn_c, kn_x], axis=1),
                               jnp.concatenate([kr_c, kr_x], axis=1),
                               jnp.concatenate([v_c, v_x], axis=1))
        mix_w = (gm_ln_g[l], gm_ln_b[l], w_s[l], b_s[l], w_conv[l], w_oa[l], w_ob[l], w_oc[l], w_o[l])
        y = merge_branches(p, attn, *mix_w)
        if not last:
            qn_c, qr_c = mla_queries(pc[..., :OFF_KV], g_q[l], w_uq[l], None)
            attn_c = block_attention(qn_c, qr_c, kn_c, kr_c, v_c)
            yc = merge_branches(pc, attn_c, *mix_w)
            xc = post_norm_residual(xc, _seg(mc, 2), yc, ln1_g[l], ln1_b[l])
        x = post_norm_residual(x, _seg(m, 2), y, ln1_g[l], ln1_b[l])

        moe_w = (w_group[l], b_group[l], w_expert[l], b_expert[l], w1[l], w3[l], w2[l])
        t = modulate(x, _seg(m, 3), _seg(m, 4)).reshape(bsz * seq, d)
        if not last:
            tc = modulate(xc, _seg(mc, 3), _seg(mc, 4)).reshape(-1, d)
            f_all = hier_moe(jnp.concatenate([t, tc], axis=0), *moe_w)
            xc = post_norm_residual(xc, _seg(mc, 5), f_all[bsz * seq:].reshape(xc.shape), ln2_g[l], ln2_b[l])
            f = f_all[:bsz * seq]
        else:
            f = hier_moe(t, *moe_w)
        x = post_norm_residual(x, _seg(m, 5), f.reshape(bsz, seq, d), ln2_g[l], ln2_b[l])
    return x
```

```python
import functools

import jax
import jax.numpy as jnp
from jax import lax
from jax.experimental import pallas as pl
from jax.experimental.pallas import tpu as pltpu

F32 = jnp.float32
BF16 = jnp.bfloat16

GRID_W = 64
NOPE_DIM = 128
ROPE_DIM = 64
V_DIM = 128
ROPE_BASE = 10000.0
LN_EPS = 1e-6
HEAD_W = NOPE_DIM + 2 * ROPE_DIM
META_W = 128

CFG = dict(
    tm=256,
    tq=512,
    tm_merge=512,
    tn_merge=512,
    tn_mod=1024,
    tme=256,
)
VMEM_LIMIT = 56 * 1024 * 1024


def _params(*sem):
    return pltpu.CompilerParams(dimension_semantics=sem, vmem_limit_bytes=VMEM_LIMIT)


def _dot(a, b):
    return jnp.dot(a, b, preferred_element_type=F32)


def _ln(x):
    mu = jnp.mean(x, axis=-1, keepdims=True)
    xc = x - mu
    var = jnp.mean(xc * xc, axis=-1, keepdims=True)
    return xc * lax.rsqrt(var + LN_EPS)


def _rms(x):
    return x * lax.rsqrt(jnp.mean(x * x, axis=-1, keepdims=True) + LN_EPS)


def _pack_rows(t):
    half = t.shape[1] // 2
    lo = lax.bitcast_convert_type(t[:, :half].astype(BF16).astype(F32), jnp.uint32)
    hi = lax.bitcast_convert_type(t[:, half:].astype(BF16).astype(F32), jnp.uint32)
    return (hi & jnp.uint32(0xFFFF0000)) | (lo >> 16)


def _unpack_rows(xp):
    lo = lax.bitcast_convert_type(xp << 16, F32).astype(BF16)
    hi = lax.bitcast_convert_type(xp & jnp.uint32(0xFFFF0000), F32).astype(BF16)
    return lo, hi


def _packed_width(d):
    return d // 2


def _mod_kernel(c_ref, w_ref, b_ref, o_ref):
    c = c_ref[...]
    a = (c * jax.nn.sigmoid(c)).astype(BF16)
    o_ref[...] = _dot(a, w_ref[...].astype(BF16)) + b_ref[...]


def _mod_vectors(cc, w_mod, b_mod):
    nl, d, d6 = w_mod.shape
    rows = cc.shape[0]
    tn = CFG["tn_mod"]
    return pl.pallas_call(
        _mod_kernel,
        grid=(nl, d6 // tn),
        in_specs=[
            pl.BlockSpec((rows, d), lambda l, n: (0, 0)),
            pl.BlockSpec((None, d, tn), lambda l, n: (l, 0, n)),
            pl.BlockSpec((None, 1, tn), lambda l, n: (l, 0, n)),
        ],
        out_specs=pl.BlockSpec((None, rows, tn), lambda l, n: (l, 0, n)),
        out_shape=jax.ShapeDtypeStruct((nl, rows, d6), F32),
        compiler_params=_params("parallel", "parallel"),
        name="mod_vectors",
    )(cc, w_mod, b_mod.reshape(nl, 1, d6))


class _Rows:
    def __init__(self, bsz, seq, ctx_len, tm):
        assert seq % tm == 0 and ctx_len % tm == 0
        self.bsz, self.seq, self.ctx_len, self.tm = bsz, seq, ctx_len, tm
        self.nsb = seq // tm
        self.ncb = ctx_len // tm
        self.nlat = bsz * self.nsb
        self.nctx = bsz * self.ncb
        self.nall = self.nlat + self.nctx

    def mod_row(self, i):
        return jnp.where(i < self.nlat, i // self.nsb, self.bsz)

    def mod_spec(self, d, seg):
        return pl.BlockSpec((None, 1, d), lambda i: (self.mod_row(i), 0, seg))

    def rope_spec(self):
        return pl.BlockSpec((self.tm, 2 * ROPE_DIM), lambda i: (jnp.where(i < self.nlat, i % self.nsb, self.nsb), 0))

    def two_source_specs(self, d):
        return [
            pl.BlockSpec((self.tm, d), lambda i: (jnp.minimum(i, self.nlat - 1), 0)),
            pl.BlockSpec((self.tm, d), lambda i: (jnp.maximum(i - self.nlat, 0), 0)),
        ]


def _row_spec(tm, width):
    return pl.BlockSpec((tm, width), lambda i: (i, 0))


def _const_spec(shape):
    return pl.BlockSpec(shape, lambda i: (0,) * len(shape))


def _lnmod_kernel(xa_ref, xb_ref, sh_ref, sc_ref, h_ref, *, nlat):
    x = jnp.where(pl.program_id(0) < nlat, xa_ref[...], xb_ref[...])
    h_ref[...] = (_ln(x) * (1.0 + sc_ref[...]) + sh_ref[...]).astype(BF16)


def _ln_modulate(rows, xa, xb, m3):
    d = xa.shape[1]
    return pl.pallas_call(
        functools.partial(_lnmod_kernel, nlat=rows.nlat),
        grid=(rows.nall,),
        in_specs=rows.two_source_specs(d) + [rows.mod_spec(d, 0), rows.mod_spec(d, 1)],
        out_specs=_row_spec(rows.tm, d),
        out_shape=jax.ShapeDtypeStruct((rows.nall * rows.tm, d), BF16),
        compiler_params=_params("parallel"),
        name="ln_modulate",
    )(xa, xb, m3, m3)


def _rope(x2, cs):
    w = x2 * cs
    return w + pltpu.roll(w, ROPE_DIM, axis=1)


def _q_kernel(h_ref, wqa_ref, gq_ref, wuq_ref, cs_ref, q_ref, *, nheads):
    a = _rms(_dot(h_ref[...], wqa_ref[...])) * gq_ref[...]
    q = _dot(a.astype(BF16), wuq_ref[...])
    cs = cs_ref[...]
    for hh in range(nheads):
        base = hh * HEAD_W
        q_ref[:, base:base + NOPE_DIM] = q[:, base:base + NOPE_DIM].astype(BF16)
        q_ref[:, base + NOPE_DIM:base + HEAD_W] = _rope(q[:, base + NOPE_DIM:base + HEAD_W], cs).astype(BF16)


def _mla_queries(rows, nblk, h, w_qa, g_q, w_uq, cs, nheads):
    d, ql = w_qa.shape
    return pl.pallas_call(
        functools.partial(_q_kernel, nheads=nheads),
        grid=(nblk,),
        in_specs=[
            _row_spec(rows.tm, d),
            _const_spec((d, ql)),
            _const_spec((1, ql)),
            _const_spec((ql, nheads * HEAD_W)),
            rows.rope_spec(),
        ],
        out_specs=_row_spec(rows.tm, nheads * HEAD_W),
        out_shape=jax.ShapeDtypeStruct((nblk * rows.tm, nheads * HEAD_W), BF16),
        compiler_params=_params("parallel"),
        name="mla_queries",
    )(h, w_qa, g_q, w_uq, cs)


def _kv_kernel(h_ref, wkva_ref, gkv_ref, wuk_ref, wuv_ref, cs_ref, k_ref, v_ref, *, nheads, kvl):
    a = _dot(h_ref[...], wkva_ref[...])
    c = (_rms(a[:, :kvl]) * gkv_ref[...]).astype(BF16)
    kr = _rope(a[:, kvl:], cs_ref[...])
    lane = lax.broadcasted_iota(jnp.int32, kr.shape, 1)
    kr = jnp.where(lane < ROPE_DIM, kr, 0.0).astype(BF16)
    kn = _dot(c, wuk_ref[...]).astype(BF16)
    v_ref[...] = _dot(c, wuv_ref[...]).astype(BF16)
    for hh in range(nheads):
        k_ref[:, hh * HEAD_W:hh * HEAD_W + NOPE_DIM] = kn[:, hh * NOPE_DIM:(hh + 1) * NOPE_DIM]
        k_ref[:, hh * HEAD_W + NOPE_DIM:(hh + 1) * HEAD_W] = kr


def _mla_keys_values(rows, h, w_kva, g_kv, w_uk, w_uv, cs, nheads):
    d, kw = w_kva.shape
    kvl = kw - 2 * ROPE_DIM
    per_batch = rows.ncb + rows.nsb

    def out_block(i):
        j = i - rows.nlat
        lat = (i // rows.nsb) * per_batch + rows.ncb + i % rows.nsb
        ctx = (j // rows.ncb) * per_batch + j % rows.ncb
        return jnp.where(i < rows.nlat, lat, ctx)

    nrows = rows.nall * rows.tm
    return pl.pallas_call(
        functools.partial(_kv_kernel, nheads=nheads, kvl=kvl),
        grid=(rows.nall,),
        in_specs=[
            _row_spec(rows.tm, d),
            _const_spec((d, kw)),
            _const_spec((1, kvl)),
            _const_spec((kvl, nheads * NOPE_DIM)),
            _const_spec((kvl, nheads * V_DIM)),
            rows.rope_spec(),
        ],
        out_specs=[
            pl.BlockSpec((rows.tm, nheads * HEAD_W), lambda i: (out_block(i), 0)),
            pl.BlockSpec((rows.tm, nheads * V_DIM), lambda i: (out_block(i), 0)),
        ],
        out_shape=[
            jax.ShapeDtypeStruct((nrows, nheads * HEAD_W), BF16),
            jax.ShapeDtypeStruct((nrows, nheads * V_DIM), BF16),
        ],
        compiler_params=_params("parallel"),
        name="mla_keys_values",
    )(h, w_kva, g_kv, w_uk, w_uv, cs)


def _attn_kernel(q_ref, k_ref, v_ref, *rest, scale):
    o_ref = rest[-1]
    s = lax.dot_general(q_ref[...], k_ref[...], (((1,), (1,)), ((), ())), preferred_element_type=F32) * scale
    p = jnp.exp(s - jnp.max(s, axis=-1, keepdims=True))
    denom = jnp.sum(p, axis=-1, keepdims=True)
    o_ref[...] = (_dot(p.astype(BF16), v_ref[...]) / denom).astype(BF16)


def _attention(q, k, v, bsz, nheads, lq, tq, q_block0, lk, k_block_stride, out_rows, into=None):
    nq = lq // tq
    in_specs = [
        pl.BlockSpec((tq, HEAD_W), lambda b, hh, qi: (q_block0 + b * nq + qi, hh)),
        pl.BlockSpec((lk, HEAD_W), lambda b, hh, qi: (b * k_block_stride, hh)),
        pl.BlockSpec((lk, V_DIM), lambda b, hh, qi: (b * k_block_stride, hh)),
    ]
    args = [q, k, v]
    aliases = {}
    if into is not None:
        in_specs.append(pl.BlockSpec(memory_space=pl.ANY))
        args.append(into)
        aliases = {3: 0}
    return pl.pallas_call(
        functools.partial(_attn_kernel, scale=float(NOPE_DIM + ROPE_DIM) ** -0.5),
        grid=(bsz, nheads, nq),
        in_specs=in_specs,
        out_specs=pl.BlockSpec((tq, V_DIM), lambda b, hh, qi: (q_block0 + b * nq + qi, hh)),
        out_shape=jax.ShapeDtypeStruct((out_rows, nheads * V_DIM), BF16),
        input_output_aliases=aliases,
        compiler_params=_params("parallel", "parallel", "parallel"),
        name="attention",
    )(*args)


def _gmlp_kernel(h_ref, w_ref, lg_ref, lb_ref, ws_ref, bs_ref, o_ref, *, width, chunk, groups):
    z = jax.nn.gelu(_dot(h_ref[...], w_ref[...]))
    u = z[:, :width]
    v = (_ln(z[:, width:]) * lg_ref[...] + lb_ref[...]).astype(BF16)
    gw = width // groups
    for ci in range(z.shape[0] // chunk):
        r0 = ci * chunk
        for g in range(groups):
            c0 = g * gw
            mixed = _dot(ws_ref[g], v[r0:r0 + chunk, c0:c0 + gw]) + bs_ref[g]
            o_ref[r0:r0 + chunk, c0:c0 + gw] = (u[r0:r0 + chunk, c0:c0 + gw] * mixed).astype(BF16)


def _chunk_gmlp(rows, nblk, h, w_gm, ln_g, ln_b, w_s, bs_exp):
    d, w2 = w_gm.shape
    width = w2 // 2
    groups, chunk, _ = w_s.shape
    assert rows.tm % chunk == 0
    return pl.pallas_call(
        functools.partial(_gmlp_kernel, width=width, chunk=chunk, groups=groups),
        grid=(nblk,),
        in_specs=[
            _row_spec(rows.tm, d),
            _const_spec((d, w2)),
            _const_spec((1, width)),
            _const_spec((1, width)),
            _const_spec((groups, chunk, chunk)),
            _const_spec((groups, chunk, width // groups)),
        ],
        out_specs=_row_spec(rows.tm, width),
        out_shape=jax.ShapeDtypeStruct((nblk * rows.tm, width), BF16),
        compiler_params=_params("parallel"),
        name="chunk_gmlp",
    )(h, w_gm, ln_g, ln_b, w_s, bs_exp)


HALO = 16


def _conv_kernel(h_ref, hp_ref, hn_ref, w_ref, wc_ref, o_ref, *, cw, rows):
    i = pl.program_id(0)
    j = i - rows.nlat
    first = jnp.where(i < rows.nlat, i % rows.nsb == 0, j % rows.ncb == 0)
    last = jnp.where(i < rows.nlat, i % rows.nsb == rows.nsb - 1, j % rows.ncb == rows.ncb - 1)
    w = w_ref[...]

    def gated(hb):
        z = _dot(hb, w)
        return z[:, :cw], z[:, cw:2 * cw] * z[:, 2 * cw:]

    gate_b, y = gated(h_ref[...])
    y_prev = jnp.where(first, 0.0, gated(hp_ref[...])[1][HALO - 1:HALO, :])
    y_next = jnp.where(last, 0.0, gated(hn_ref[...])[1][0:1, :])
    tm = y.shape[0]
    row = lax.broadcasted_iota(jnp.int32, y.shape, 0)
    y_dn = jnp.where(row == 0, y_prev, pltpu.roll(y, 1, axis=0))
    y_up = jnp.where(row == tm - 1, y_next, pltpu.roll(y, tm - 1, axis=0))
    wc = wc_ref[...]
    conv = y_dn * wc[0:1, :] + y * wc[1:2, :] + y_up * wc[2:3, :]
    o_ref[...] = (gate_b * conv).astype(BF16)


def _short_conv(rows, nblk, h, w_cv, w_conv):
    d, w3 = w_cv.shape
    cw = w3 // 3
    per = rows.tm // HALO
    nhalo = h.shape[0] // HALO
    return pl.pallas_call(
        functools.partial(_conv_kernel, cw=cw, rows=rows),
        grid=(nblk,),
        in_specs=[
            _row_spec(rows.tm, d),
            pl.BlockSpec((HALO, d), lambda i: (jnp.maximum(i * per - 1, 0), 0)),
            pl.BlockSpec((HALO, d), lambda i: (jnp.minimum((i + 1) * per, nhalo - 1), 0)),
            _const_spec((d, w3)),
            _const_spec(w_conv.shape),
        ],
        out_specs=_row_spec(rows.tm, cw),
        out_shape=jax.ShapeDtypeStruct((nblk * rows.tm, cw), BF16),
        compiler_params=_params("parallel"),
        name="short_conv",
    )(h, h, h, w_cv, w_conv)


def _merge_kernel(h_ref, a_ref, gm_ref, cv_ref, wg0_ref, wg1_ref, wg2_ref, woa_ref, wob_ref, woc_ref, o_ref):
    h = h_ref[...]
    m = jax.nn.sigmoid(_dot(h, wg0_ref[...])) * _dot(a_ref[...], woa_ref[...])
    m = m + jax.nn.sigmoid(_dot(h, wg1_ref[...])) * _dot(gm_ref[...], wob_ref[...])
    m = m + jax.nn.sigmoid(_dot(h, wg2_ref[...])) * _dot(cv_ref[...], woc_ref[...])
    o_ref[...] = m.astype(BF16)


def _merge(nrows, h, attn, gm, cv, w_gate, w_oa, w_ob, w_oc):
    d = h.shape[1]
    tm, tn = CFG["tm_merge"], CFG["tn_merge"]
    ncol = d // tn

    def act(width):
        return pl.BlockSpec((tm, width), lambda n, i: (i, 0))

    def gate_w(k):
        return pl.BlockSpec((d, tn), lambda n, i: (0, k * ncol + n))

    def out_w(width):
        return pl.BlockSpec((width, tn), lambda n, i: (0, n))

    return pl.pallas_call(
        _merge_kernel,
        grid=(ncol, nrows // tm),
        in_specs=[act(d), act(attn.shape[1]), act(gm.shape[1]), act(cv.shape[1]),
                  gate_w(0), gate_w(1), gate_w(2),
                  out_w(w_oa.shape[0]), out_w(w_ob.shape[0]), out_w(w_oc.shape[0])],
        out_specs=pl.BlockSpec((tm, tn), lambda n, i: (i, n)),
        out_shape=jax.ShapeDtypeStruct((nrows, d), BF16),
        compiler_params=_params("parallel", "parallel"),
        name="merge_branches",
    )(h, attn, gm, cv, w_gate, w_gate, w_gate, w_oa, w_ob, w_oc)


def _route(logits, carry, ne, ng):
    tm = logits.shape[0]
    neg = -1e30
    lane = lax.broadcasted_iota(jnp.int32, logits.shape, 1).astype(F32)
    big = float(4 * META_W)

    def first_lane(mask):
        return jnp.min(jnp.where(mask, lane, big), axis=-1, keepdims=True)

    gl = jnp.where((lane >= ne) & (lane < ne + ng), logits, neg)
    gmax = jnp.max(gl, axis=-1, keepdims=True)
    g_w = 1.0 / jnp.sum(jnp.exp(gl - gmax), axis=-1, keepdims=True)
    g_idx = first_lane(gl == gmax) - ne
    epg = ne // ng
    in_group = (lane >= g_idx * epg) & (lane < (g_idx + 1) * epg)
    el = jnp.where(in_group, logits, neg)
    ee = jnp.where(in_group, jnp.exp(el - jnp.max(el, axis=-1, keepdims=True)), -1.0)
    v1 = jnp.max(ee, axis=-1, keepdims=True)
    i1 = first_lane(ee == v1)
    ee2 = jnp.where(lane == i1, -1.0, ee)
    v2 = jnp.max(ee2, axis=-1, keepdims=True)
    i2 = first_lane(ee2 == v2)
    w1 = g_w * v1 / (v1 + v2)
    w2 = g_w * v2 / (v1 + v2)

    hit1 = lane == i1
    hit2 = lane == i2
    onehot = jnp.where(hit1 | hit2, 1.0, 0.0)
    r_i = lax.broadcasted_iota(jnp.int32, (tm, tm), 0)
    c_i = lax.broadcasted_iota(jnp.int32, (tm, tm), 1)
    earlier = jnp.where(r_i > c_i, 1.0, 0.0).astype(BF16)
    base = carry + _dot(earlier, onehot.astype(BF16))
    rank1 = jnp.sum(jnp.where(hit1, base, 0.0), axis=-1, keepdims=True)
    rank2 = jnp.sum(jnp.where(hit2, base, 0.0), axis=-1, keepdims=True)
    new_carry = carry + jnp.sum(onehot, axis=0, keepdims=True)

    rec = jnp.zeros(logits.shape, F32)
    for k, val in enumerate((i1, i2, rank1, rank2, w1, w2)):
        rec = jnp.where(lane == float(k), val, rec)
    return rec, new_carry


def _post1_kernel(mg_ref, xa_ref, xb_ref, wo_ref, gate_ref, sh_ref, sc_ref, lg_ref, lb_ref, wr_ref, br_ref,
                  x1_ref, t_ref, meta_ref, cnt_ref, carry_ref, *, nlat, alpha, ne, ng):
    i = pl.program_id(0)

    @pl.when(i == 0)
    def _():
        carry_ref[...] = jnp.zeros(carry_ref.shape, F32)

    x = jnp.where(i < nlat, xa_ref[...], xb_ref[...])
    y = _dot(mg_ref[...], wo_ref[...])
    x1 = _ln(alpha * x + gate_ref[...] * y) * lg_ref[...] + lb_ref[...]
    x1_ref[...] = x1
    t = _ln(x1) * (1.0 + sc_ref[...]) + sh_ref[...]
    t_ref[...] = _pack_rows(t)
    logits = _dot(t.astype(BF16), wr_ref[...]) + br_ref[...]
    rec, new_carry = _route(logits, carry_ref[0:1, :], ne, ng)
    meta_ref[...] = rec
    carry_ref[...] = jnp.broadcast_to(new_carry, carry_ref.shape)
    cnt_ref[...] = carry_ref[...]


def _post_mixer(rows, nblk, merged, xa, xb, w_o, m3, ln_g, ln_b, w_r, b_r, alpha, ne, ng):
    d = merged.shape[1]
    tm = rows.tm
    nrows = nblk * tm
    pw = _packed_width(d)
    return pl.pallas_call(
        functools.partial(_post1_kernel, nlat=rows.nlat, alpha=alpha, ne=ne, ng=ng),
        grid=(nblk,),
        in_specs=[_row_spec(tm, d)] + rows.two_source_specs(d) + [
            _const_spec((d, d)),
            rows.mod_spec(d, 2), rows.mod_spec(d, 3), rows.mod_spec(d, 4),
            _const_spec((1, d)), _const_spec((1, d)),
            _const_spec((d, META_W)), _const_spec((1, META_W)),
        ],
        out_specs=[_row_spec(tm, d), _row_spec(tm, pw), _row_spec(tm, META_W), _const_spec((8, META_W))],
        out_shape=[
            jax.ShapeDtypeStruct((nrows, d), F32),
            jax.ShapeDtypeStruct((nrows, pw), jnp.uint32),
            jax.ShapeDtypeStruct((nrows, META_W), F32),
            jax.ShapeDtypeStruct((8, META_W), F32),
        ],
        scratch_shapes=[pltpu.VMEM((8, META_W), F32)],
        compiler_params=_params("arbitrary"),
        name="post_mixer_router",
    )(merged, xa, xb, w_o, m3, m3, m3, ln_g, ln_b, w_r, b_r)


def _dispatch_kernel(dest_ref, t_ref, zero_ref, buf_ref, sem, *, tm):
    del zero_ref
    i = pl.program_id(0)

    def copy(r, d):
        return pltpu.make_async_copy(t_ref.at[pl.ds(r, 1)], buf_ref.at[pl.ds(d, 1)], sem)

    def start(r, carry):
        a = (i * tm + r) * 2
        copy(r, dest_ref[a]).start()
        copy(r, dest_ref[a + 1]).start()
        return carry

    def wait(r, carry):
        copy(0, 0).wait()
        copy(0, 0).wait()
        return carry

    lax.fori_loop(0, tm, start, 0)
    lax.fori_loop(0, tm, wait, 0)


def _dispatch(dest, t_packed, buf_rows, tm):
    nrows, pw = t_packed.shape
    zeros = jnp.zeros((buf_rows, pw), t_packed.dtype)
    return pl.pallas_call(
        functools.partial(_dispatch_kernel, tm=tm),
        grid_spec=pltpu.PrefetchScalarGridSpec(
            num_scalar_prefetch=1,
            grid=(nrows // tm,),
            in_specs=[pl.BlockSpec((tm, pw), lambda i, dest: (i, 0)),
                      pl.BlockSpec(memory_space=pl.ANY)],
            out_specs=pl.BlockSpec(memory_space=pl.ANY),
            scratch_shapes=[pltpu.SemaphoreType.DMA(())],
        ),
        out_shape=jax.ShapeDtypeStruct((buf_rows, pw), t_packed.dtype),
        input_output_aliases={2: 0},
        compiler_params=_params("arbitrary"),
        name="moe_dispatch",
    )(dest, t_packed, zeros)


def _expert_up_kernel(be_ref, nu_ref, x_ref, w1_ref, w3_ref, h_ref):
    del be_ref

    @pl.when(pl.program_id(0) < nu_ref[0])
    def _():
        lo, hi = _unpack_rows(x_ref[...])
        half = lo.shape[1]

        def proj(w_ref):
            return (_dot(lo, w_ref[:half, :].astype(BF16)) + _dot(hi, w_ref[half:, :].astype(BF16)))

        a = proj(w1_ref)
        h_ref[...] = (a * jax.nn.sigmoid(a) * proj(w3_ref)).astype(BF16)


def _expert_down_kernel(be_ref, nu_ref, h_ref, w2_ref, y_ref):
    del be_ref

    @pl.when(pl.program_id(0) < nu_ref[0])
    def _():
        y_ref[...] = _dot(h_ref[...], w2_ref[...].astype(BF16))


def _experts(block_e, n_used, buf, w1, w3, w2, layer):
    tme = CFG["tme"]
    nb = buf.shape[0] // tme
    pw = buf.shape[1]
    _, _, d, eh = w1.shape

    def row_block(i, be, nu):
        return (jnp.minimum(i, nu[0] - 1), 0)

    def weight_block(i, be, nu):
        return (layer, be[i], 0, 0)

    hid = pl.pallas_call(
        _expert_up_kernel,
        grid_spec=pltpu.PrefetchScalarGridSpec(
            num_scalar_prefetch=2,
            grid=(nb,),
            in_specs=[pl.BlockSpec((tme, pw), row_block),
                      pl.BlockSpec((None, None, d, eh), weight_block),
                      pl.BlockSpec((None, None, d, eh), weight_block)],
            out_specs=pl.BlockSpec((tme, eh), row_block),
        ),
        out_shape=jax.ShapeDtypeStruct((nb * tme, eh), BF16),
        compiler_params=_params("arbitrary"),
        name="expert_up",
    )(block_e, n_used, buf, w1, w3)
    return pl.pallas_call(
        _expert_down_kernel,
        grid_spec=pltpu.PrefetchScalarGridSpec(
            num_scalar_prefetch=2,
            grid=(nb,),
            in_specs=[pl.BlockSpec((tme, eh), row_block),
                      pl.BlockSpec((None, None, eh, d), weight_block)],
            out_specs=pl.BlockSpec((tme, d), row_block),
        ),
        out_shape=jax.ShapeDtypeStruct((nb * tme, d), F32),
        compiler_params=_params("arbitrary"),
        name="expert_down",
    )(block_e, n_used, hid, w2)


def _combine_kernel(dest_ref, y_ref, meta_ref, x1_ref, gate_ref, lg_ref, lb_ref, sh_ref, sc_ref, *rest,
                    tm, alpha, emit_h):
    if emit_h:
        x2_ref, h_ref, gbuf, sem = rest
    else:
        x2_ref, gbuf, sem = rest
    i = pl.program_id(0)
    nsteps = pl.num_programs(0)

    def copy(src_row, slot, k, r):
        return pltpu.make_async_copy(y_ref.at[pl.ds(src_row, 1)], gbuf.at[slot, k, pl.ds(r, 1)], sem.at[slot])

    def issue(blk, slot):
        def body(r, carry):
            a = (blk * tm + r) * 2
            copy(dest_ref[a], slot, 0, r).start()
            copy(dest_ref[a + 1], slot, 1, r).start()
            return carry
        lax.fori_loop(0, tm, body, 0)

    @pl.when(i == 0)
    def _():
        issue(0, 0)

    @pl.when(i + 1 < nsteps)
    def _():
        issue(i + 1, (i + 1) % 2)

    slot = i % 2

    def wait(r, carry):
        copy(0, slot, 0, 0).wait()
        copy(0, slot, 1, 0).wait()
        return carry
    lax.fori_loop(0, tm, wait, 0)

    meta = meta_ref[...]
    f = meta[:, 4:5] * gbuf[slot, 0] + meta[:, 5:6] * gbuf[slot, 1]
    x2 = _ln(alpha * x1_ref[...] + gate_ref[...] * f) * lg_ref[...] + lb_ref[...]
    x2_ref[...] = x2
    if emit_h:
        h_ref[...] = (_ln(x2) * (1.0 + sc_ref[...]) + sh_ref[...]).astype(BF16)


def _combine(rows, nblk, dest, ybuf, meta, x1, m3, ln_g, ln_b, m3_next, alpha):
    d = x1.shape[1]
    tm = rows.tm
    nrows = nblk * tm
    emit_h = m3_next is not None
    if not emit_h:
        m3_next = m3

    def rspec(width):
        return pl.BlockSpec((tm, width), lambda i, dest: (i, 0))

    def cspec(shape):
        return pl.BlockSpec(shape, lambda i, dest: (0,) * len(shape))

    def mspec(seg):
        return pl.BlockSpec((None, 1, d), lambda i, dest: (rows.mod_row(i), 0, seg))

    out_specs = [rspec(d)]
    out_shape = [jax.ShapeDtypeStruct((nrows, d), F32)]
    if emit_h:
        out_specs.append(rspec(d))
        out_shape.append(jax.ShapeDtypeStruct((nrows, d), BF16))
    out = pl.pallas_call(
        functools.partial(_combine_kernel, tm=tm, alpha=alpha, emit_h=emit_h),
        grid_spec=pltpu.PrefetchScalarGridSpec(
            num_scalar_prefetch=1,
            grid=(nblk,),
            in_specs=[pl.BlockSpec(memory_space=pl.ANY), rspec(META_W), rspec(d), mspec(5),
                      cspec((1, d)), cspec((1, d)), mspec(0), mspec(1)],
            out_specs=out_specs,
            scratch_shapes=[pltpu.VMEM((2, 2, tm, d), F32), pltpu.SemaphoreType.DMA((2,))],
        ),
        out_shape=out_shape,
        compiler_params=_params("arbitrary"),
        name="moe_combine",
    )(dest, ybuf, meta, x1, m3, ln_g, ln_b, m3_next, m3_next)
    return out if emit_h else (out[0], None)


def _rot_cols(w):
    q = ROPE_DIM // 4
    return jnp.concatenate([-w[..., q:2 * q], w[..., 0:q], -w[..., 3 * q:4 * q], w[..., 2 * q:3 * q]], axis=-1)


def _rope_table(seq, tm):
    n_rows = seq // GRID_W
    t = jnp.arange(n_rows * GRID_W)
    row = (t // GRID_W).astype(F32)
    col = (t % GRID_W).astype(F32)
    half = ROPE_DIM // 2
    inv = ROPE_BASE ** (-jnp.arange(0, half, 2, dtype=F32) / half)
    ang = jnp.concatenate([row[:, None] * inv] * 2 + [col[:, None] * inv] * 2, axis=-1)
    cs = jnp.concatenate([jnp.cos(ang), jnp.sin(ang)], axis=-1)
    ident = jnp.concatenate([jnp.ones((tm, ROPE_DIM), F32), jnp.zeros((tm, ROPE_DIM), F32)], axis=-1)
    return jnp.concatenate([cs, ident], axis=0)


def kernel(x, c, ctx, c_ctx, w_mod, b_mod, w_in, g_q, w_uq, g_kv, w_ukv, w_oa, gm_ln_g, gm_ln_b, w_s, b_s, w_ob,
           w_conv, w_oc, w_o, ln1_g, ln1_b, w_group, b_group, w_expert, b_expert, w1, w3, w2, ln2_g, ln2_b):
    bsz, seq, d = x.shape
    ctx_len = ctx.shape[1]
    depth = w_mod.shape[0]
    ql = g_q.shape[1]
    kvl = g_kv.shape[1]
    nheads = w_uq.shape[2] // (NOPE_DIM + ROPE_DIM)
    gwidth = gm_ln_g.shape[1]
    cwidth = w_conv.shape[2]
    ne = w_expert.shape[2]
    ng = w_group.shape[2]
    alpha = float((2 * depth) ** 0.25)
    tm, tq, tme = CFG["tm"], CFG["tq"], CFG["tme"]
    rows = _Rows(bsz, seq, ctx_len, tm)
    n_lat, n_ctx = bsz * seq, bsz * ctx_len
    off_kv = ql
    off_gm = off_kv + kvl + ROPE_DIM
    off_cv = off_gm + 2 * gwidth
    off_gate = off_cv + 3 * cwidth

    mod_rows = -(-(bsz + 1) // 8) * 8
    cc = jnp.zeros((mod_rows, d), F32).at[:bsz].set(c).at[bsz].set(c_ctx)
    m_all = _mod_vectors(cc, w_mod, b_mod)
    cs = _rope_table(seq, tm)

    xa = x.reshape(n_lat, d)
    xb = ctx.reshape(n_ctx, d)
    h = None
    for l in range(depth):
        last = l == depth - 1
        m3 = m_all[l].reshape(mod_rows, 1, 6 * d)
        nblk = rows.nlat if last else rows.nall
        nrows = nblk * tm

        wl = w_in[l]
        w_qa = wl[:, :off_kv].astype(BF16)
        kr_cols = wl[:, off_kv + kvl:off_gm]
        w_kva = jnp.concatenate([wl[:, off_kv:off_kv + kvl], kr_cols, _rot_cols(kr_cols)], axis=1).astype(BF16)
        w_gm = wl[:, off_gm:off_cv].astype(BF16)
        w_cv = wl[:, off_cv:off_gate].astype(BF16)
        w_gate = wl[:, off_gate:].astype(BF16)
        uq = w_uq[l].reshape(ql, nheads, NOPE_DIM + ROPE_DIM)
        uq_rope = uq[..., NOPE_DIM:]
        w_uq_l = jnp.concatenate([uq, _rot_cols(uq_rope)], axis=-1).reshape(ql, nheads * HEAD_W).astype(BF16)
        ukv = w_ukv[l].reshape(kvl, nheads, NOPE_DIM + V_DIM)
        w_uk = ukv[..., :NOPE_DIM].reshape(kvl, nheads * NOPE_DIM).astype(BF16)
        w_uv = ukv[..., NOPE_DIM:].reshape(kvl, nheads * V_DIM).astype(BF16)
        bs_exp = jnp.broadcast_to(b_s[l][:, :, None], b_s.shape[1:] + (gwidth // w_s.shape[1],))
        w_r = jnp.zeros((d, META_W), F32).at[:, :ne].set(w_expert[l]).at[:, ne:ne + ng].set(w_group[l]).astype(BF16)
        b_r = jnp.zeros((1, META_W), F32).at[0, :ne].set(b_expert[l]).at[0, ne:ne + ng].set(b_group[l])

        if l == 0:
            h = _ln_modulate(rows, xa, xb, m3)

        q = _mla_queries(rows, nblk, h, w_qa, g_q[l][None], w_uq_l, cs, nheads)
        kcat, vcat = _mla_keys_values(rows, h, w_kva, g_kv[l][None], w_uk, w_uv, cs, nheads)
        lk = ctx_len + seq
        attn = _attention(q, kcat, vcat, bsz, nheads, seq, min(tq, seq), 0, lk, 1, nrows)
        if not last:
            assert lk % ctx_len == 0
            tqc = min(tq, ctx_len)
            attn = _attention(q, kcat, vcat, bsz, nheads, ctx_len, tqc, n_lat // tqc, ctx_len, lk // ctx_len, nrows,
                              into=attn)
        gm = _chunk_gmlp(rows, nblk, h, w_gm, gm_ln_g[l][None], gm_ln_b[l][None], w_s[l].astype(BF16), bs_exp)
        cv = _short_conv(rows, nblk, h, w_cv, w_conv[l])
        merged = _merge(nrows, h, attn, gm, cv, w_gate, w_oa[l].astype(BF16), w_ob[l].astype(BF16),
                        w_oc[l].astype(BF16))
        x1, t_packed, meta, cnt = _post_mixer(rows, nblk, merged, xa, xb, w_o[l].astype(BF16), m3, ln1_g[l][None],
                                              ln1_b[l][None], w_r, b_r, alpha, ne, ng)

        e_ids = meta[:, 0:2].astype(jnp.int32)
        ranks = meta[:, 2:4].astype(jnp.int32)
        counts = cnt[0, :ne].astype(jnp.int32)
        padded = (counts + tme - 1) // tme * tme
        pend = jnp.cumsum(padded)
        dest = ((pend - padded)[e_ids] + ranks).reshape(-1)
        nb = 2 * nrows // tme + ne
        n_used = pend[-1:] // tme
        blk = jnp.minimum(jnp.arange(nb, dtype=jnp.int32), n_used[0] - 1)
        block_e = jnp.clip(jnp.searchsorted(pend, blk * tme, side="right"), 0, ne - 1).astype(jnp.int32)

        buf = _dispatch(dest, t_packed, nb * tme, tm)
        ybuf = _experts(block_e, n_used.astype(jnp.int32), buf, w1, w3, w2, l)
        m3_next = None if last else m_all[l + 1].reshape(mod_rows, 1, 6 * d)
        x2, h = _combine(rows, nblk, dest, ybuf, meta, x1, m3, ln2_g[l][None], ln2_b[l][None], m3_next, alpha)
        xa, xb = x2, x2
    return xa.reshape(bsz, seq, d)
```

```python
import functools

import jax
import jax.numpy as jnp
from jax import lax
from jax.experimental import pallas as pl
from jax.experimental.pallas import tpu as pltpu

F32 = jnp.float32
BF16 = jnp.bfloat16

GRID_W = 64
NOPE_DIM = 128
ROPE_DIM = 64
V_DIM = 128
ROPE_BASE = 10000.0
LN_EPS = 1e-6
HEAD_W = NOPE_DIM + 2 * ROPE_DIM
META_W = 128

CFG = dict(
    tm=256,
    tq=2048,
    tq_sub=256,
    tm_merge=512,
    tn_merge=512,
    tn_mod=1024,
    tme=256,
)
VMEM_LIMIT = 56 * 1024 * 1024


def _params(*sem):
    return pltpu.CompilerParams(dimension_semantics=sem, vmem_limit_bytes=VMEM_LIMIT)


def _dot(a, b):
    return jnp.dot(a, b, preferred_element_type=F32)


def _ln(x):
    mu = jnp.mean(x, axis=-1, keepdims=True)
    xc = x - mu
    var = jnp.mean(xc * xc, axis=-1, keepdims=True)
    return xc * lax.rsqrt(var + LN_EPS)


def _rms(x):
    return x * lax.rsqrt(jnp.mean(x * x, axis=-1, keepdims=True) + LN_EPS)


def _pack_rows(t):
    half = t.shape[1] // 2
    return pltpu.pack_elementwise([t[:, :half], t[:, half:]], packed_dtype=BF16)


def _unpack_rows(xp):
    lo = pltpu.unpack_elementwise(xp, index=0, packed_dtype=BF16, unpacked_dtype=F32)
    hi = pltpu.unpack_elementwise(xp, index=1, packed_dtype=BF16, unpacked_dtype=F32)
    return lo.astype(BF16), hi.astype(BF16)


def _packed_width(d):
    return d // 2


def _mod_kernel(c_ref, w_ref, b_ref, o_ref):
    c = c_ref[...]
    a = (c * jax.nn.sigmoid(c)).astype(BF16)
    o_ref[...] = _dot(a, w_ref[...].astype(BF16)) + b_ref[...]


def _mod_vectors(cc, w_mod, b_mod):
    nl, d, d6 = w_mod.shape
    rows = cc.shape[0]
    tn = CFG["tn_mod"]
    return pl.pallas_call(
        _mod_kernel,
        grid=(nl, d6 // tn),
        in_specs=[
            pl.BlockSpec((rows, d), lambda l, n: (0, 0)),
            pl.BlockSpec((None, d, tn), lambda l, n: (l, 0, n)),
            pl.BlockSpec((None, 1, tn), lambda l, n: (l, 0, n)),
        ],
        out_specs=pl.BlockSpec((None, rows, tn), lambda l, n: (l, 0, n)),
        out_shape=jax.ShapeDtypeStruct((nl, rows, d6), F32),
        compiler_params=_params("parallel", "parallel"),
        name="mod_vectors",
    )(cc, w_mod, b_mod.reshape(nl, 1, d6))


class _Rows:
    def __init__(self, bsz, seq, ctx_len, tm):
        assert seq % tm == 0 and ctx_len % tm == 0
        self.bsz, self.seq, self.ctx_len, self.tm = bsz, seq, ctx_len, tm
        self.nsb = seq // tm
        self.ncb = ctx_len // tm
        self.nlat = bsz * self.nsb
        self.nctx = bsz * self.ncb
        self.nall = self.nlat + self.nctx

    def mod_row(self, i):
        return jnp.where(i < self.nlat, i // self.nsb, self.bsz)

    def mod_spec(self, d, seg):
        return pl.BlockSpec((None, 1, d), lambda i: (self.mod_row(i), 0, seg))

    def rope_spec(self):
        return pl.BlockSpec((self.tm, 2 * ROPE_DIM), lambda i: (jnp.where(i < self.nlat, i % self.nsb, self.nsb), 0))

    def two_source_specs(self, d):
        return [
            pl.BlockSpec((self.tm, d), lambda i: (jnp.minimum(i, self.nlat - 1), 0)),
            pl.BlockSpec((self.tm, d), lambda i: (jnp.maximum(i - self.nlat, 0), 0)),
        ]


def _row_spec(tm, width):
    return pl.BlockSpec((tm, width), lambda i: (i, 0))


def _const_spec(shape):
    return pl.BlockSpec(shape, lambda i: (0,) * len(shape))


def _lnmod_kernel(xa_ref, xb_ref, sh_ref, sc_ref, h_ref, *, nlat):
    x = jnp.where(pl.program_id(0) < nlat, xa_ref[...], xb_ref[...])
    h_ref[...] = (_ln(x) * (1.0 + sc_ref[...]) + sh_ref[...]).astype(BF16)


def _ln_modulate(rows, xa, xb, m3):
    d = xa.shape[1]
    return pl.pallas_call(
        functools.partial(_lnmod_kernel, nlat=rows.nlat),
        grid=(rows.nall,),
        in_specs=rows.two_source_specs(d) + [rows.mod_spec(d, 0), rows.mod_spec(d, 1)],
        out_specs=_row_spec(rows.tm, d),
        out_shape=jax.ShapeDtypeStruct((rows.nall * rows.tm, d), BF16),
        compiler_params=_params("parallel"),
        name="ln_modulate",
    )(xa, xb, m3, m3)


def _rope(x2, cs):
    w = x2 * cs
    return w + pltpu.roll(w, ROPE_DIM, axis=1)


def _q_kernel(h_ref, wqa_ref, gq_ref, wuq_ref, cs_ref, q_ref, *, nheads):
    a = _rms(_dot(h_ref[...], wqa_ref[...])) * gq_ref[...]
    q = _dot(a.astype(BF16), wuq_ref[...])
    cs = cs_ref[...]
    for hh in range(nheads):
        base = hh * HEAD_W
        q_ref[:, base:base + NOPE_DIM] = q[:, base:base + NOPE_DIM].astype(BF16)
        q_ref[:, base + NOPE_DIM:base + HEAD_W] = _rope(q[:, base + NOPE_DIM:base + HEAD_W], cs).astype(BF16)


def _mla_queries(rows, nblk, h, w_qa, g_q, w_uq, cs, nheads):
    d, ql = w_qa.shape
    return pl.pallas_call(
        functools.partial(_q_kernel, nheads=nheads),
        grid=(nblk,),
        in_specs=[
            _row_spec(rows.tm, d),
            _const_spec((d, ql)),
            _const_spec((1, ql)),
            _const_spec((ql, nheads * HEAD_W)),
            rows.rope_spec(),
        ],
        out_specs=_row_spec(rows.tm, nheads * HEAD_W),
        out_shape=jax.ShapeDtypeStruct((nblk * rows.tm, nheads * HEAD_W), BF16),
        compiler_params=_params("parallel"),
        name="mla_queries",
    )(h, w_qa, g_q, w_uq, cs)


def _kv_kernel(h_ref, wkva_ref, gkv_ref, wuk_ref, wuv_ref, cs_ref, k_ref, v_ref, *, nheads, kvl):
    a = _dot(h_ref[...], wkva_ref[...])
    c = (_rms(a[:, :kvl]) * gkv_ref[...]).astype(BF16)
    kr = _rope(a[:, kvl:], cs_ref[...])
    lane = lax.broadcasted_iota(jnp.int32, kr.shape, 1)
    kr = jnp.where(lane < ROPE_DIM, kr, 0.0).astype(BF16)
    kn = _dot(c, wuk_ref[...]).astype(BF16)
    v_ref[...] = _dot(c, wuv_ref[...]).astype(BF16)
    for hh in range(nheads):
        k_ref[:, hh * HEAD_W:hh * HEAD_W + NOPE_DIM] = kn[:, hh * NOPE_DIM:(hh + 1) * NOPE_DIM]
        k_ref[:, hh * HEAD_W + NOPE_DIM:(hh + 1) * HEAD_W] = kr


def _mla_keys_values(rows, h, w_kva, g_kv, w_uk, w_uv, cs, nheads):
    d, kw = w_kva.shape
    kvl = kw - 2 * ROPE_DIM
    per_batch = rows.ncb + rows.nsb

    def out_block(i):
        j = i - rows.nlat
        lat = (i // rows.nsb) * per_batch + rows.ncb + i % rows.nsb
        ctx = (j // rows.ncb) * per_batch + j % rows.ncb
        return jnp.where(i < rows.nlat, lat, ctx)

    nrows = rows.nall * rows.tm
    return pl.pallas_call(
        functools.partial(_kv_kernel, nheads=nheads, kvl=kvl),
        grid=(rows.nall,),
        in_specs=[
            _row_spec(rows.tm, d),
            _const_spec((d, kw)),
            _const_spec((1, kvl)),
            _const_spec((kvl, nheads * NOPE_DIM)),
            _const_spec((kvl, nheads * V_DIM)),
            rows.rope_spec(),
        ],
        out_specs=[
            pl.BlockSpec((rows.tm, nheads * HEAD_W), lambda i: (out_block(i), 0)),
            pl.BlockSpec((rows.tm, nheads * V_DIM), lambda i: (out_block(i), 0)),
        ],
        out_shape=[
            jax.ShapeDtypeStruct((nrows, nheads * HEAD_W), BF16),
            jax.ShapeDtypeStruct((nrows, nheads * V_DIM), BF16),
        ],
        compiler_params=_params("parallel"),
        name="mla_keys_values",
    )(h, w_kva, g_kv, w_uk, w_uv, cs)


def _attn_kernel(q_ref, k_ref, v_ref, *rest, scale, sub):
    o_ref = rest[-1]
    k = k_ref[...]
    v = v_ref[...]
    for r0 in range(0, q_ref.shape[0], sub):
        s = lax.dot_general(q_ref[r0:r0 + sub, :], k, (((1,), (1,)), ((), ())), preferred_element_type=F32) * scale
        p = jnp.exp(s - jnp.max(s, axis=-1, keepdims=True))
        denom = jnp.sum(p, axis=-1, keepdims=True)
        o_ref[r0:r0 + sub, :] = (_dot(p.astype(BF16), v) / denom).astype(BF16)


def _attention(q, k, v, bsz, nheads, lq, tq, q_block0, lk, k_block_stride, out_rows, into=None):
    nq = lq // tq
    in_specs = [
        pl.BlockSpec((tq, HEAD_W), lambda b, hh, qi: (q_block0 + b * nq + qi, hh)),
        pl.BlockSpec((lk, HEAD_W), lambda b, hh, qi: (b * k_block_stride, hh)),
        pl.BlockSpec((lk, V_DIM), lambda b, hh, qi: (b * k_block_stride, hh)),
    ]
    args = [q, k, v]
    aliases = {}
    if into is not None:
        in_specs.append(pl.BlockSpec(memory_space=pl.ANY))
        args.append(into)
        aliases = {3: 0}
    return pl.pallas_call(
        functools.partial(_attn_kernel, scale=float(NOPE_DIM + ROPE_DIM) ** -0.5, sub=min(tq, CFG["tq_sub"])),
        grid=(bsz, nheads, nq),
        in_specs=in_specs,
        out_specs=pl.BlockSpec((tq, V_DIM), lambda b, hh, qi: (q_block0 + b * nq + qi, hh)),
        out_shape=jax.ShapeDtypeStruct((out_rows, nheads * V_DIM), BF16),
        input_output_aliases=aliases,
        compiler_params=_params("parallel", "parallel", "parallel"),
        name="attention",
    )(*args)


def _gmlp_kernel(h_ref, w_ref, lg_ref, lb_ref, ws_ref, bs_ref, o_ref, *, width, chunk, groups):
    z = jax.nn.gelu(_dot(h_ref[...], w_ref[...]))
    u = z[:, :width]
    v = (_ln(z[:, width:]) * lg_ref[...] + lb_ref[...]).astype(BF16)
    gw = width // groups
    for ci in range(z.shape[0] // chunk):
        r0 = ci * chunk
        for g in range(groups):
            c0 = g * gw
            mixed = _dot(ws_ref[g], v[r0:r0 + chunk, c0:c0 + gw]) + bs_ref[g]
            o_ref[r0:r0 + chunk, c0:c0 + gw] = (u[r0:r0 + chunk, c0:c0 + gw] * mixed).astype(BF16)


def _chunk_gmlp(rows, nblk, h, w_gm, ln_g, ln_b, w_s, bs_exp):
    d, w2 = w_gm.shape
    width = w2 // 2
    groups, chunk, _ = w_s.shape
    assert rows.tm % chunk == 0
    return pl.pallas_call(
        functools.partial(_gmlp_kernel, width=width, chunk=chunk, groups=groups),
        grid=(nblk,),
        in_specs=[
            _row_spec(rows.tm, d),
            _const_spec((d, w2)),
            _const_spec((1, width)),
            _const_spec((1, width)),
            _const_spec((groups, chunk, chunk)),
            _const_spec((groups, chunk, width // groups)),
        ],
        out_specs=_row_spec(rows.tm, width),
        out_shape=jax.ShapeDtypeStruct((nblk * rows.tm, width), BF16),
        compiler_params=_params("parallel"),
        name="chunk_gmlp",
    )(h, w_gm, ln_g, ln_b, w_s, bs_exp)


HALO = 16


def _conv_kernel(h_ref, hp_ref, hn_ref, w_ref, wc_ref, o_ref, *, cw, rows):
    i = pl.program_id(0)
    j = i - rows.nlat
    first = jnp.where(i < rows.nlat, i % rows.nsb == 0, j % rows.ncb == 0)
    last = jnp.where(i < rows.nlat, i % rows.nsb == rows.nsb - 1, j % rows.ncb == rows.ncb - 1)
    w = w_ref[...]

    def gated(hb):
        z = _dot(hb, w)
        return z[:, :cw], z[:, cw:2 * cw] * z[:, 2 * cw:]

    gate_b, y = gated(h_ref[...])
    y_prev = jnp.where(first, 0.0, gated(hp_ref[...])[1][HALO - 1:HALO, :])
    y_next = jnp.where(last, 0.0, gated(hn_ref[...])[1][0:1, :])
    tm = y.shape[0]
    row = lax.broadcasted_iota(jnp.int32, y.shape, 0)
    y_dn = jnp.where(row == 0, y_prev, pltpu.roll(y, 1, axis=0))
    y_up = jnp.where(row == tm - 1, y_next, pltpu.roll(y, tm - 1, axis=0))
    wc = wc_ref[...]
    conv = y_dn * wc[0:1, :] + y * wc[1:2, :] + y_up * wc[2:3, :]
    o_ref[...] = (gate_b * conv).astype(BF16)


def _short_conv(rows, nblk, h, w_cv, w_conv):
    d, w3 = w_cv.shape
    cw = w3 // 3
    per = rows.tm // HALO
    nhalo = h.shape[0] // HALO
    return pl.pallas_call(
        functools.partial(_conv_kernel, cw=cw, rows=rows),
        grid=(nblk,),
        in_specs=[
            _row_spec(rows.tm, d),
            pl.BlockSpec((HALO, d), lambda i: (jnp.maximum(i * per - 1, 0), 0)),
            pl.BlockSpec((HALO, d), lambda i: (jnp.minimum((i + 1) * per, nhalo - 1), 0)),
            _const_spec((d, w3)),
            _const_spec(w_conv.shape),
        ],
        out_specs=_row_spec(rows.tm, cw),
        out_shape=jax.ShapeDtypeStruct((nblk * rows.tm, cw), BF16),
        compiler_params=_params("parallel"),
        name="short_conv",
    )(h, h, h, w_cv, w_conv)


def _merge_kernel(h_ref, a_ref, gm_ref, cv_ref, wg0_ref, wg1_ref, wg2_ref, woa_ref, wob_ref, woc_ref, o_ref):
    h = h_ref[...]
    m = jax.nn.sigmoid(_dot(h, wg0_ref[...])) * _dot(a_ref[...], woa_ref[...])
    m = m + jax.nn.sigmoid(_dot(h, wg1_ref[...])) * _dot(gm_ref[...], wob_ref[...])
    m = m + jax.nn.sigmoid(_dot(h, wg2_ref[...])) * _dot(cv_ref[...], woc_ref[...])
    o_ref[...] = m.astype(BF16)


def _merge(nrows, h, attn, gm, cv, w_gate, w_oa, w_ob, w_oc):
    d = h.shape[1]
    tm, tn = CFG["tm_merge"], CFG["tn_merge"]
    ncol = d // tn

    def act(width):
        return pl.BlockSpec((tm, width), lambda n, i: (i, 0))

    def gate_w(k):
        return pl.BlockSpec((d, tn), lambda n, i: (0, k * ncol + n))

    def out_w(width):
        return pl.BlockSpec((width, tn), lambda n, i: (0, n))

    return pl.pallas_call(
        _merge_kernel,
        grid=(ncol, nrows // tm),
        in_specs=[act(d), act(attn.shape[1]), act(gm.shape[1]), act(cv.shape[1]),
                  gate_w(0), gate_w(1), gate_w(2),
                  out_w(w_oa.shape[0]), out_w(w_ob.shape[0]), out_w(w_oc.shape[0])],
        out_specs=pl.BlockSpec((tm, tn), lambda n, i: (i, n)),
        out_shape=jax.ShapeDtypeStruct((nrows, d), BF16),
        compiler_params=_params("parallel", "parallel"),
        name="merge_branches",
    )(h, attn, gm, cv, w_gate, w_gate, w_gate, w_oa, w_ob, w_oc)


def _route(logits, carry, ne, ng):
    tm = logits.shape[0]
    neg = -1e30
    lane = lax.broadcasted_iota(jnp.int32, logits.shape, 1).astype(F32)
    big = float(4 * META_W)

    def first_lane(mask):
        return jnp.min(jnp.where(mask, lane, big), axis=-1, keepdims=True)

    gl = jnp.where((lane >= ne) & (lane < ne + ng), logits, neg)
    gmax = jnp.max(gl, axis=-1, keepdims=True)
    g_w = 1.0 / jnp.sum(jnp.exp(gl - gmax), axis=-1, keepdims=True)
    g_idx = first_lane(gl == gmax) - ne
    epg = ne // ng
    in_group = (lane >= g_idx * epg) & (lane < (g_idx + 1) * epg)
    el = jnp.where(in_group, logits, neg)
    ee = jnp.where(in_group, jnp.exp(el - jnp.max(el, axis=-1, keepdims=True)), -1.0)
    v1 = jnp.max(ee, axis=-1, keepdims=True)
    i1 = first_lane(ee == v1)
    ee2 = jnp.where(lane == i1, -1.0, ee)
    v2 = jnp.max(ee2, axis=-1, keepdims=True)
    i2 = first_lane(ee2 == v2)
    w1 = g_w * v1 / (v1 + v2)
    w2 = g_w * v2 / (v1 + v2)

    hit1 = lane == i1
    hit2 = lane == i2
    onehot = jnp.where(hit1 | hit2, 1.0, 0.0)
    r_i = lax.broadcasted_iota(jnp.int32, (tm, tm), 0)
    c_i = lax.broadcasted_iota(jnp.int32, (tm, tm), 1)
    earlier = jnp.where(r_i > c_i, 1.0, 0.0).astype(BF16)
    base = carry + _dot(earlier, onehot.astype(BF16))
    rank1 = jnp.sum(jnp.where(hit1, base, 0.0), axis=-1, keepdims=True)
    rank2 = jnp.sum(jnp.where(hit2, base, 0.0), axis=-1, keepdims=True)
    new_carry = carry + jnp.sum(onehot, axis=0, keepdims=True)

    rec = jnp.zeros(logits.shape, F32)
    for k, val in enumerate((i1, i2, rank1, rank2, w1, w2)):
        rec = jnp.where(lane == float(k), val, rec)
    return rec, new_carry


def _post1_kernel(mg_ref, xa_ref, xb_ref, wo_ref, gate_ref, sh_ref, sc_ref, lg_ref, lb_ref, wr_ref, br_ref,
                  x1_ref, t_ref, meta_ref, cnt_ref, carry_ref, *, nlat, alpha, ne, ng):
    i = pl.program_id(0)

    @pl.when(i == 0)
    def _():
        carry_ref[...] = jnp.zeros(carry_ref.shape, F32)

    x = jnp.where(i < nlat, xa_ref[...], xb_ref[...])
    y = _dot(mg_ref[...], wo_ref[...])
    x1 = _ln(alpha * x + gate_ref[...] * y) * lg_ref[...] + lb_ref[...]
    x1_ref[...] = x1
    t = _ln(x1) * (1.0 + sc_ref[...]) + sh_ref[...]
    t_ref[...] = _pack_rows(t)
    logits = _dot(t.astype(BF16), wr_ref[...]) + br_ref[...]
    rec, new_carry = _route(logits, carry_ref[0:1, :], ne, ng)
    meta_ref[...] = rec
    carry_ref[...] = jnp.broadcast_to(new_carry, carry_ref.shape)
    cnt_ref[...] = carry_ref[...]


def _post_mixer(rows, nblk, merged, xa, xb, w_o, m3, ln_g, ln_b, w_r, b_r, alpha, ne, ng):
    d = merged.shape[1]
    tm = rows.tm
    nrows = nblk * tm
    pw = _packed_width(d)
    return pl.pallas_call(
        functools.partial(_post1_kernel, nlat=rows.nlat, alpha=alpha, ne=ne, ng=ng),
        grid=(nblk,),
        in_specs=[_row_spec(tm, d)] + rows.two_source_specs(d) + [
            _const_spec((d, d)),
            rows.mod_spec(d, 2), rows.mod_spec(d, 3), rows.mod_spec(d, 4),
            _const_spec((1, d)), _const_spec((1, d)),
            _const_spec((d, META_W)), _const_spec((1, META_W)),
        ],
        out_specs=[_row_spec(tm, d), _row_spec(tm, pw), _row_spec(tm, META_W), _const_spec((8, META_W))],
        out_shape=[
            jax.ShapeDtypeStruct((nrows, d), F32),
            jax.ShapeDtypeStruct((nrows, pw), jnp.uint32),
            jax.ShapeDtypeStruct((nrows, META_W), F32),
            jax.ShapeDtypeStruct((8, META_W), F32),
        ],
        scratch_shapes=[pltpu.VMEM((8, META_W), F32)],
        compiler_params=_params("arbitrary"),
        name="post_mixer_router",
    )(merged, xa, xb, w_o, m3, m3, m3, ln_g, ln_b, w_r, b_r)


ISSUE_UNROLL = 8


def _dispatch_kernel(dest_ref, tail_ref, t_ref, buf_ref, zero_ref, sem, zsem, *, tm, tme, ne):
    i = pl.program_id(0)

    @pl.when(i == 0)
    def _():
        zero_ref[...] = jnp.zeros(zero_ref.shape, zero_ref.dtype)

        def zero_copy(e):
            row0 = pl.multiple_of(jnp.maximum(tail_ref[e], 0) * tme, tme)
            return pltpu.make_async_copy(zero_ref, buf_ref.at[pl.ds(row0, tme)], zsem)

        for e in range(ne):
            @pl.when(tail_ref[e] >= 0)
            def _():
                zero_copy(e).start()
        for e in range(ne):
            @pl.when(tail_ref[e] >= 0)
            def _():
                zero_copy(e).wait()

    def start(r, carry):
        a = (i * tm + r) * 2
        for k in range(2):
            pltpu.make_async_copy(t_ref.at[pl.ds(r, 1)], buf_ref.at[pl.ds(dest_ref[a + k], 1)], sem).start()
        return carry

    lax.fori_loop(0, tm, start, 0, unroll=ISSUE_UNROLL)
    for k in range(2):
        pltpu.make_async_copy(t_ref, buf_ref.at[pl.ds(0, tm)], sem).wait()


def _dispatch(dest, tail_blk, t_packed, buf_rows, tm, tme):
    nrows, pw = t_packed.shape
    return pl.pallas_call(
        functools.partial(_dispatch_kernel, tm=tm, tme=tme, ne=tail_blk.shape[0]),
        grid_spec=pltpu.PrefetchScalarGridSpec(
            num_scalar_prefetch=2,
            grid=(nrows // tm,),
            in_specs=[pl.BlockSpec((tm, pw), lambda i, dest, tail: (i, 0))],
            out_specs=pl.BlockSpec(memory_space=pl.ANY),
            scratch_shapes=[pltpu.VMEM((tme, pw), t_packed.dtype),
                            pltpu.SemaphoreType.DMA(()), pltpu.SemaphoreType.DMA(())],
        ),
        out_shape=jax.ShapeDtypeStruct((buf_rows, pw), t_packed.dtype),
        compiler_params=_params("arbitrary"),
        name="moe_dispatch",
    )(dest, tail_blk, t_packed)


def _expert_kernel(be_ref, nu_ref, nxt_ref, x_ref, w1_hbm, w3_hbm, w2_hbm, y_ref,
                   s1_ref, s3_ref, s2_ref, b1_ref, b3_ref, b2_ref, sem, *, layer):
    i = pl.program_id(0)

    def fetch(e):
        return (pltpu.make_async_copy(w1_hbm.at[layer, e], s1_ref, sem.at[0]),
                pltpu.make_async_copy(w3_hbm.at[layer, e], s3_ref, sem.at[1]),
                pltpu.make_async_copy(w2_hbm.at[layer, e], s2_ref, sem.at[2]))

    @pl.when(i < nu_ref[0])
    def _():
        e = be_ref[i]
        run_start = jnp.logical_or(i == 0, e != be_ref[jnp.maximum(i - 1, 0)])

        @pl.when(i == 0)
        def _():
            for cp in fetch(e):
                cp.start()

        @pl.when(run_start)
        def _():
            for cp in fetch(e):
                cp.wait()
            b1_ref[...] = s1_ref[...].astype(BF16)
            b3_ref[...] = s3_ref[...].astype(BF16)
            b2_ref[...] = s2_ref[...].astype(BF16)
            nxt = nxt_ref[i]

            @pl.when(nxt >= 0)
            def _():
                for cp in fetch(nxt):
                    cp.start()

        lo, hi = _unpack_rows(x_ref[...])
        half = lo.shape[1]

        def proj(b_ref):
            return _dot(lo, b_ref[:half, :]) + _dot(hi, b_ref[half:, :])

        a = proj(b1_ref)
        hid = (a * jax.nn.sigmoid(a) * proj(b3_ref)).astype(BF16)
        y_ref[...] = _dot(hid, b2_ref[...])


def _experts(block_e, n_used, next_e, buf, w1, w3, w2, layer):
    tme = CFG["tme"]
    nb = buf.shape[0] // tme
    pw = buf.shape[1]
    _, _, d, eh = w1.shape

    def row_block(i, be, nu, nxt):
        return (jnp.minimum(i, nu[0] - 1), 0)

    hbm = pl.BlockSpec(memory_space=pl.ANY)
    return pl.pallas_call(
        functools.partial(_expert_kernel, layer=layer),
        grid_spec=pltpu.PrefetchScalarGridSpec(
            num_scalar_prefetch=3,
            grid=(nb,),
            in_specs=[pl.BlockSpec((tme, pw), row_block), hbm, hbm, hbm],
            out_specs=pl.BlockSpec((tme, d), row_block),
            scratch_shapes=[pltpu.VMEM((d, eh), F32), pltpu.VMEM((d, eh), F32), pltpu.VMEM((eh, d), F32),
                            pltpu.VMEM((d, eh), BF16), pltpu.VMEM((d, eh), BF16), pltpu.VMEM((eh, d), BF16),
                            pltpu.SemaphoreType.DMA((3,))],
        ),
        out_shape=jax.ShapeDtypeStruct((nb * tme, d), F32),
        compiler_params=_params("arbitrary"),
        name="experts",
    )(block_e, n_used, next_e, buf, w1, w3, w2)


def _combine_kernel(dest_ref, y_ref, meta_ref, x1_ref, gate_ref, lg_ref, lb_ref, sh_ref, sc_ref, *rest,
                    tm, alpha, emit_h):
    if emit_h:
        x2_ref, h_ref, gbuf, sem = rest
    else:
        x2_ref, gbuf, sem = rest
    i = pl.program_id(0)
    nsteps = pl.num_programs(0)

    def copy(src_row, slot, k, r):
        return pltpu.make_async_copy(y_ref.at[pl.ds(src_row, 1)], gbuf.at[slot, k, pl.ds(r, 1)], sem.at[slot])

    def issue(blk, slot):
        def body(r, carry):
            a = (blk * tm + r) * 2
            copy(dest_ref[a], slot, 0, r).start()
            copy(dest_ref[a + 1], slot, 1, r).start()
            return carry
        lax.fori_loop(0, tm, body, 0, unroll=ISSUE_UNROLL)

    @pl.when(i == 0)
    def _():
        issue(0, 0)

    @pl.when(i + 1 < nsteps)
    def _():
        issue(i + 1, (i + 1) % 2)

    slot = i % 2
    for k in range(2):
        pltpu.make_async_copy(y_ref.at[pl.ds(0, tm)], gbuf.at[slot, k], sem.at[slot]).wait()

    meta = meta_ref[...]
    f = meta[:, 4:5] * gbuf[slot, 0] + meta[:, 5:6] * gbuf[slot, 1]
    x2 = _ln(alpha * x1_ref[...] + gate_ref[...] * f) * lg_ref[...] + lb_ref[...]
    x2_ref[...] = x2
    if emit_h:
        h_ref[...] = (_ln(x2) * (1.0 + sc_ref[...]) + sh_ref[...]).astype(BF16)


def _combine(rows, nblk, dest, ybuf, meta, x1, m3, ln_g, ln_b, m3_next, alpha):
    d = x1.shape[1]
    tm = rows.tm
    nrows = nblk * tm
    emit_h = m3_next is not None
    if not emit_h:
        m3_next = m3

    def rspec(width):
        return pl.BlockSpec((tm, width), lambda i, dest: (i, 0))

    def cspec(shape):
        return pl.BlockSpec(shape, lambda i, dest: (0,) * len(shape))

    def mspec(seg):
        return pl.BlockSpec((None, 1, d), lambda i, dest: (rows.mod_row(i), 0, seg))

    out_specs = [rspec(d)]
    out_shape = [jax.ShapeDtypeStruct((nrows, d), F32)]
    if emit_h:
        out_specs.append(rspec(d))
        out_shape.append(jax.ShapeDtypeStruct((nrows, d), BF16))
    out = pl.pallas_call(
        functools.partial(_combine_kernel, tm=tm, alpha=alpha, emit_h=emit_h),
        grid_spec=pltpu.PrefetchScalarGridSpec(
            num_scalar_prefetch=1,
            grid=(nblk,),
            in_specs=[pl.BlockSpec(memory_space=pl.ANY), rspec(META_W), rspec(d), mspec(5),
                      cspec((1, d)), cspec((1, d)), mspec(0), mspec(1)],
            out_specs=out_specs,
            scratch_shapes=[pltpu.VMEM((2, 2, tm, d), F32), pltpu.SemaphoreType.DMA((2,))],
        ),
        out_shape=out_shape,
        compiler_params=_params("arbitrary"),
        name="moe_combine",
    )(dest, ybuf, meta, x1, m3, ln_g, ln_b, m3_next, m3_next)
    return out if emit_h else (out[0], None)


def _rot_cols(w):
    q = ROPE_DIM // 4
    return jnp.concatenate([-w[..., q:2 * q], w[..., 0:q], -w[..., 3 * q:4 * q], w[..., 2 * q:3 * q]], axis=-1)


def _rope_table(seq, tm):
    n_rows = seq // GRID_W
    t = jnp.arange(n_rows * GRID_W)
    row = (t // GRID_W).astype(F32)
    col = (t % GRID_W).astype(F32)
    half = ROPE_DIM // 2
    inv = ROPE_BASE ** (-jnp.arange(0, half, 2, dtype=F32) / half)
    ang = jnp.concatenate([row[:, None] * inv] * 2 + [col[:, None] * inv] * 2, axis=-1)
    cs = jnp.concatenate([jnp.cos(ang), jnp.sin(ang)], axis=-1)
    ident = jnp.concatenate([jnp.ones((tm, ROPE_DIM), F32), jnp.zeros((tm, ROPE_DIM), F32)], axis=-1)
    return jnp.concatenate([cs, ident], axis=0)


def kernel(x, c, ctx, c_ctx, w_mod, b_mod, w_in, g_q, w_uq, g_kv, w_ukv, w_oa, gm_ln_g, gm_ln_b, w_s, b_s, w_ob,
           w_conv, w_oc, w_o, ln1_g, ln1_b, w_group, b_group, w_expert, b_expert, w1, w3, w2, ln2_g, ln2_b):
    bsz, seq, d = x.shape
    ctx_len = ctx.shape[1]
    depth = w_mod.shape[0]
    ql = g_q.shape[1]
    kvl = g_kv.shape[1]
    nheads = w_uq.shape[2] // (NOPE_DIM + ROPE_DIM)
    gwidth = gm_ln_g.shape[1]
    cwidth = w_conv.shape[2]
    ne = w_expert.shape[2]
    ng = w_group.shape[2]
    alpha = float((2 * depth) ** 0.25)
    tm, tq, tme = CFG["tm"], CFG["tq"], CFG["tme"]
    rows = _Rows(bsz, seq, ctx_len, tm)
    n_lat, n_ctx = bsz * seq, bsz * ctx_len
    off_kv = ql
    off_gm = off_kv + kvl + ROPE_DIM
    off_cv = off_gm + 2 * gwidth
    off_gate = off_cv + 3 * cwidth

    mod_rows = -(-(bsz + 1) // 8) * 8
    cc = jnp.zeros((mod_rows, d), F32).at[:bsz].set(c).at[bsz].set(c_ctx)
    m_all = _mod_vectors(cc, w_mod, b_mod)
    cs = _rope_table(seq, tm)

    xa = x.reshape(n_lat, d)
    xb = ctx.reshape(n_ctx, d)
    h = None
    for l in range(depth):
        last = l == depth - 1
        m3 = m_all[l].reshape(mod_rows, 1, 6 * d)
        nblk = rows.nlat if last else rows.nall
        nrows = nblk * tm

        wl = w_in[l]
        w_qa = wl[:, :off_kv].astype(BF16)
        kr_cols = wl[:, off_kv + kvl:off_gm]
        w_kva = jnp.concatenate([wl[:, off_kv:off_kv + kvl], kr_cols, _rot_cols(kr_cols)], axis=1).astype(BF16)
        w_gm = wl[:, off_gm:off_cv].astype(BF16)
        w_cv = wl[:, off_cv:off_gate].astype(BF16)
        w_gate = wl[:, off_gate:].astype(BF16)
        uq = w_uq[l].reshape(ql, nheads, NOPE_DIM + ROPE_DIM)
        uq_rope = uq[..., NOPE_DIM:]
        w_uq_l = jnp.concatenate([uq, _rot_cols(uq_rope)], axis=-1).reshape(ql, nheads * HEAD_W).astype(BF16)
        ukv = w_ukv[l].reshape(kvl, nheads, NOPE_DIM + V_DIM)
        w_uk = ukv[..., :NOPE_DIM].reshape(kvl, nheads * NOPE_DIM).astype(BF16)
        w_uv = ukv[..., NOPE_DIM:].reshape(kvl, nheads * V_DIM).astype(BF16)
        bs_exp = jnp.broadcast_to(b_s[l][:, :, None], b_s.shape[1:] + (gwidth // w_s.shape[1],))
        w_r = jnp.zeros((d, META_W), F32).at[:, :ne].set(w_expert[l]).at[:, ne:ne + ng].set(w_group[l]).astype(BF16)
        b_r = jnp.zeros((1, META_W), F32).at[0, :ne].set(b_expert[l]).at[0, ne:ne + ng].set(b_group[l])

        if l == 0:
            h = _ln_modulate(rows, xa, xb, m3)

        q = _mla_queries(rows, nblk, h, w_qa, g_q[l][None], w_uq_l, cs, nheads)
        kcat, vcat = _mla_keys_values(rows, h, w_kva, g_kv[l][None], w_uk, w_uv, cs, nheads)
        lk = ctx_len + seq
        attn = _attention(q, kcat, vcat, bsz, nheads, seq, min(tq, seq), 0, lk, 1, nrows)
        if not last:
            assert lk % ctx_len == 0
            tqc = min(tq, ctx_len)
            attn = _attention(q, kcat, vcat, bsz, nheads, ctx_len, tqc, n_lat // tqc, ctx_len, lk // ctx_len, nrows,
                              into=attn)
        gm = _chunk_gmlp(rows, nblk, h, w_gm, gm_ln_g[l][None], gm_ln_b[l][None], w_s[l].astype(BF16), bs_exp)
        cv = _short_conv(rows, nblk, h, w_cv, w_conv[l])
        merged = _merge(nrows, h, attn, gm, cv, w_gate, w_oa[l].astype(BF16), w_ob[l].astype(BF16),
                        w_oc[l].astype(BF16))
        x1, t_packed, meta, cnt = _post_mixer(rows, nblk, merged, xa, xb, w_o[l].astype(BF16), m3, ln1_g[l][None],
                                              ln1_b[l][None], w_r, b_r, alpha, ne, ng)

        e_ids = meta[:, 0:2].astype(jnp.int32)
        ranks = meta[:, 2:4].astype(jnp.int32)
        counts = cnt[0, :ne].astype(jnp.int32)
        experts = jnp.arange(ne, dtype=jnp.int32)
        padded = (counts + tme - 1) // tme * tme
        pend = jnp.cumsum(padded)
        pstart = pend - padded
        dest = (jnp.sum(jnp.where(e_ids[..., None] == experts, pstart, 0), axis=-1) + ranks).reshape(-1)
        nb = 2 * nrows // tme + ne
        n_used = pend[-1:] // tme
        blk = jnp.minimum(jnp.arange(nb, dtype=jnp.int32), n_used[0] - 1)
        block_e = jnp.minimum(jnp.sum(pend[None, :] <= (blk * tme)[:, None], axis=1), ne - 1).astype(jnp.int32)
        tail_blk = jnp.where(padded > 0, pend // tme - 1, -1).astype(jnp.int32)
        nonempty_from = lax.cummin(jnp.where(padded > 0, experts, ne), axis=0, reverse=True)
        next_nonempty = jnp.concatenate([nonempty_from[1:], jnp.full((1,), ne, jnp.int32)])
        next_e = next_nonempty[block_e]
        next_e = jnp.where(next_e >= ne, -1, next_e).astype(jnp.int32)

        buf = _dispatch(dest, tail_blk, t_packed, nb * tme, tm, tme)
        ybuf = _experts(block_e, n_used.astype(jnp.int32), next_e, buf, w1, w3, w2, l)
        m3_next = None if last else m_all[l + 1].reshape(mod_rows, 1, 6 * d)
        x2, h = _combine(rows, nblk, dest, ybuf, meta, x1, m3, ln2_g[l][None], ln2_b[l][None], m3_next, alpha)
        xa, xb = x2, x2
    return xa.reshape(bsz, seq, d)
```

```python
import functools

import jax
import jax.numpy as jnp
from jax import lax
from jax.experimental import pallas as pl
from jax.experimental.pallas import tpu as pltpu

F32 = jnp.float32
BF16 = jnp.bfloat16

GRID_W = 64
NOPE_DIM = 128
ROPE_DIM = 64
V_DIM = 128
ROPE_BASE = 10000.0
LN_EPS = 1e-6
HEAD_W = NOPE_DIM + 2 * ROPE_DIM
META_W = 128
LANES = 128
SUBLANES = 8

CFG = dict(
    tm=256,
    tq=2048,
    tq_sub=256,
    tm_merge=512,
    tn_merge=512,
    tn_mod=1024,
    tme=256,
    combine_chunk=32,
)
VMEM_LIMIT = 56 * 1024 * 1024


def _params(*sem):
    return pltpu.CompilerParams(dimension_semantics=sem, vmem_limit_bytes=VMEM_LIMIT)


def _dot(a, b):
    return jnp.dot(a, b, preferred_element_type=F32)


def _ln(x):
    mu = jnp.mean(x, axis=-1, keepdims=True)
    xc = x - mu
    var = jnp.mean(xc * xc, axis=-1, keepdims=True)
    return xc * lax.rsqrt(var + LN_EPS)


def _rms(x):
    return x * lax.rsqrt(jnp.mean(x * x, axis=-1, keepdims=True) + LN_EPS)


def _pack_rows(t):
    half = t.shape[1] // 2
    return pltpu.pack_elementwise([t[:, :half], t[:, half:]], packed_dtype=BF16)


def _unpack_rows(xp):
    lo = pltpu.unpack_elementwise(xp, index=0, packed_dtype=BF16, unpacked_dtype=F32)
    hi = pltpu.unpack_elementwise(xp, index=1, packed_dtype=BF16, unpacked_dtype=F32)
    return lo.astype(BF16), hi.astype(BF16)


def _packed_width(d):
    return d // 2


def _mod_kernel(c_ref, w_ref, b_ref, o_ref):
    c = c_ref[...]
    a = (c * jax.nn.sigmoid(c)).astype(BF16)
    o_ref[...] = _dot(a, w_ref[...].astype(BF16)) + b_ref[...]


def _mod_vectors(cc, w_mod, b_mod):
    nl, d, d6 = w_mod.shape
    rows = cc.shape[0]
    tn = CFG["tn_mod"]
    return pl.pallas_call(
        _mod_kernel,
        grid=(nl, d6 // tn),
        in_specs=[
            pl.BlockSpec((rows, d), lambda l, n: (0, 0)),
            pl.BlockSpec((None, d, tn), lambda l, n: (l, 0, n)),
            pl.BlockSpec((None, 1, tn), lambda l, n: (l, 0, n)),
        ],
        out_specs=pl.BlockSpec((None, rows, tn), lambda l, n: (l, 0, n)),
        out_shape=jax.ShapeDtypeStruct((nl, rows, d6), F32),
        compiler_params=_params("parallel", "parallel"),
        name="mod_vectors",
    )(cc, w_mod, b_mod.reshape(nl, 1, d6))


class _Rows:
    def __init__(self, bsz, seq, ctx_len, tm):
        assert seq % tm == 0 and ctx_len % tm == 0
        self.bsz, self.seq, self.ctx_len, self.tm = bsz, seq, ctx_len, tm
        self.nsb = seq // tm
        self.ncb = ctx_len // tm
        self.nlat = bsz * self.nsb
        self.nctx = bsz * self.ncb
        self.nall = self.nlat + self.nctx

    def mod_row(self, i):
        return jnp.where(i < self.nlat, i // self.nsb, self.bsz)

    def mod_spec(self, d, seg):
        return pl.BlockSpec((None, 1, d), lambda i: (self.mod_row(i), 0, seg))

    def rope_spec(self):
        return pl.BlockSpec((self.tm, 2 * ROPE_DIM), lambda i: (jnp.where(i < self.nlat, i % self.nsb, self.nsb), 0))

    def two_source_specs(self, d):
        return [
            pl.BlockSpec((self.tm, d), lambda i: (jnp.minimum(i, self.nlat - 1), 0)),
            pl.BlockSpec((self.tm, d), lambda i: (jnp.maximum(i - self.nlat, 0), 0)),
        ]

    def seq_edges(self, i):
        j = i - self.nlat
        first = jnp.where(i < self.nlat, i % self.nsb == 0, j % self.ncb == 0)
        last = jnp.where(i < self.nlat, i % self.nsb == self.nsb - 1, j % self.ncb == self.ncb - 1)
        return first, last


def _row_spec(tm, width):
    return pl.BlockSpec((tm, width), lambda i: (i, 0))


def _const_spec(shape):
    return pl.BlockSpec(shape, lambda i: (0,) * len(shape), pipeline_mode=pl.Buffered(1))


def _lnmod_kernel(xa_ref, xb_ref, sh_ref, sc_ref, h_ref, *, nlat):
    x = jnp.where(pl.program_id(0) < nlat, xa_ref[...], xb_ref[...])
    h_ref[...] = (_ln(x) * (1.0 + sc_ref[...]) + sh_ref[...]).astype(BF16)


def _ln_modulate(rows, xa, xb, m3):
    d = xa.shape[1]
    return pl.pallas_call(
        functools.partial(_lnmod_kernel, nlat=rows.nlat),
        grid=(rows.nall,),
        in_specs=rows.two_source_specs(d) + [rows.mod_spec(d, 0), rows.mod_spec(d, 1)],
        out_specs=_row_spec(rows.tm, d),
        out_shape=jax.ShapeDtypeStruct((rows.nall * rows.tm, d), BF16),
        compiler_params=_params("parallel"),
        name="ln_modulate",
    )(xa, xb, m3, m3)


class _InLayout:
    def __init__(self, d, ql, kvl, gwidth, cwidth, tn_gate):
        self.segs = {}
        off = 0
        for name, width, block in (("qa", ql, ql), ("ckv", kvl, kvl), ("gm", 2 * gwidth, 2 * gwidth),
                                   ("cb", cwidth, cwidth), ("cc", cwidth, cwidth), ("cx", cwidth, cwidth),
                                   ("gate", 3 * d, tn_gate), ("kr", 2 * ROPE_DIM, 2 * ROPE_DIM)):
            off = -(-off // block) * block
            self.segs[name] = (off, width, block)
            off += width
        self.width = -(-off // LANES) * LANES

    def block_index(self, name):
        off, _, block = self.segs[name]
        return off // block

    def spec(self, d, name):
        _, width, block = self.segs[name]
        assert width == block
        idx = self.block_index(name)
        return pl.BlockSpec((d, width), lambda i: (0, idx), pipeline_mode=pl.Buffered(1))


def _rope(x2, cs):
    w = x2 * cs
    return w + pltpu.roll(w, ROPE_DIM, axis=1)


def _mixer_in_kernel(h_ref, wqa_ref, wckv_ref, wkr_ref, wgm_ref, wcb_ref, wcc_ref, wcx_ref,
                     gq_ref, wuq_ref, gkv_ref, wuk_ref, wuv_ref, lg_ref, lb_ref, ws_ref, bs_ref, cs_ref,
                     q_ref, k_ref, v_ref, gm_ref, gb_ref, y_ref, *, nheads, nfull, width, chunk, groups):
    h = h_ref[...]
    cs = cs_ref[...]

    c = (_rms(_dot(h, wckv_ref[...])) * gkv_ref[...]).astype(BF16)
    kr = _rope(_dot(h, wkr_ref[...]), cs)
    lane = lax.broadcasted_iota(jnp.int32, kr.shape, 1)
    kr = jnp.where(lane < ROPE_DIM, kr, 0.0).astype(BF16)
    kn = _dot(c, wuk_ref[...]).astype(BF16)
    v_ref[...] = _dot(c, wuv_ref[...]).astype(BF16)
    for hh in range(nheads):
        k_ref[:, hh * HEAD_W:hh * HEAD_W + NOPE_DIM] = kn[:, hh * NOPE_DIM:(hh + 1) * NOPE_DIM]
        k_ref[:, hh * HEAD_W + NOPE_DIM:(hh + 1) * HEAD_W] = kr

    def rest():
        a = _rms(_dot(h, wqa_ref[...])) * gq_ref[...]
        q = _dot(a.astype(BF16), wuq_ref[...])
        for hh in range(nheads):
            base = hh * HEAD_W
            q_ref[:, base:base + NOPE_DIM] = q[:, base:base + NOPE_DIM].astype(BF16)
            q_ref[:, base + NOPE_DIM:base + HEAD_W] = _rope(q[:, base + NOPE_DIM:base + HEAD_W], cs).astype(BF16)
        z = jax.nn.gelu(_dot(h, wgm_ref[...]))
        u = z[:, :width]
        vv = (_ln(z[:, width:]) * lg_ref[...] + lb_ref[...]).astype(BF16)
        gw = width // groups
        for ci in range(z.shape[0] // chunk):
            r0 = ci * chunk
            for g in range(groups):
                c0 = g * gw
                mixed = _dot(ws_ref[g], vv[r0:r0 + chunk, c0:c0 + gw]) + bs_ref[g]
                gm_ref[r0:r0 + chunk, c0:c0 + gw] = (u[r0:r0 + chunk, c0:c0 + gw] * mixed).astype(BF16)
        gb_ref[...] = _dot(h, wcb_ref[...])
        y_ref[...] = _dot(h, wcc_ref[...]) * _dot(h, wcx_ref[...])

    if nfull is None:
        rest()
    else:
        pl.when(pl.program_id(0) < nfull)(rest)


def _mixer_in(rows, nblk, lay, h, w_all, g_q, w_uq, g_kv, w_uk, w_uv, ln_g, ln_b, w_s, bs_exp, cs, nheads):
    d = h.shape[1]
    tm = rows.tm
    ql = lay.segs["qa"][1]
    kvl = lay.segs["ckv"][1]
    width = lay.segs["gm"][1] // 2
    cw = lay.segs["cb"][1]
    groups, chunk, _ = w_s.shape
    assert tm % chunk == 0
    per_batch = rows.ncb + rows.nsb
    nrows = nblk * tm

    def kv_block(i):
        j = i - rows.nlat
        lat = (i // rows.nsb) * per_batch + rows.ncb + i % rows.nsb
        ctx = (j // rows.ncb) * per_batch + j % rows.ncb
        return jnp.where(i < rows.nlat, lat, ctx)

    def full_spec(w):
        return pl.BlockSpec((tm, w), lambda i: (jnp.minimum(i, nblk - 1), 0))

    def kv_spec(w):
        return pl.BlockSpec((tm, w), lambda i: (kv_block(i), 0))

    return pl.pallas_call(
        functools.partial(_mixer_in_kernel, nheads=nheads, nfull=None if nblk == rows.nall else nblk,
                          width=width, chunk=chunk, groups=groups),
        grid=(rows.nall,),
        in_specs=[_row_spec(tm, d)]
        + [lay.spec(d, name) for name in ("qa", "ckv", "kr", "gm", "cb", "cc", "cx")]
        + [_const_spec((1, ql)), _const_spec((ql, nheads * HEAD_W)),
           _const_spec((1, kvl)), _const_spec((kvl, nheads * NOPE_DIM)), _const_spec((kvl, nheads * V_DIM)),
           _const_spec((1, width)), _const_spec((1, width)),
           _const_spec((groups, chunk, chunk)), _const_spec((groups, chunk, width // groups)),
           rows.rope_spec()],
        out_specs=[full_spec(nheads * HEAD_W), kv_spec(nheads * HEAD_W), kv_spec(nheads * V_DIM),
                   full_spec(width), full_spec(cw), full_spec(cw)],
        out_shape=[
            jax.ShapeDtypeStruct((nrows, nheads * HEAD_W), BF16),
            jax.ShapeDtypeStruct((rows.nall * tm, nheads * HEAD_W), BF16),
            jax.ShapeDtypeStruct((rows.nall * tm, nheads * V_DIM), BF16),
            jax.ShapeDtypeStruct((nrows, width), BF16),
            jax.ShapeDtypeStruct((nrows, cw), F32),
            jax.ShapeDtypeStruct((nrows, cw), F32),
        ],
        compiler_params=_params("arbitrary"),
        name="mixer_in",
    )(h, *([w_all] * 7), g_q, w_uq, g_kv, w_uk, w_uv, ln_g, ln_b, w_s, bs_exp, cs)


def _conv_gate_kernel(gb_ref, y_ref, yp_ref, yn_ref, wc_ref, o_ref, *, rows):
    first, last = rows.seq_edges(pl.program_id(0))
    y = y_ref[...]
    y_prev = jnp.where(first, 0.0, yp_ref[SUBLANES - 1:SUBLANES, :])
    y_next = jnp.where(last, 0.0, yn_ref[0:1, :])
    tm = y.shape[0]
    row = lax.broadcasted_iota(jnp.int32, y.shape, 0)
    y_dn = jnp.where(row == 0, y_prev, pltpu.roll(y, 1, axis=0))
    y_up = jnp.where(row == tm - 1, y_next, pltpu.roll(y, tm - 1, axis=0))
    wc = wc_ref[...]
    conv = y_dn * wc[0:1, :] + y * wc[1:2, :] + y_up * wc[2:3, :]
    o_ref[...] = (gb_ref[...] * conv).astype(BF16)


def _conv_gate(rows, nblk, gate_b, y, w_conv):
    cw = y.shape[1]
    per = rows.tm // SUBLANES
    nhalo = y.shape[0] // SUBLANES
    return pl.pallas_call(
        functools.partial(_conv_gate_kernel, rows=rows),
        grid=(nblk,),
        in_specs=[
            _row_spec(rows.tm, cw),
            _row_spec(rows.tm, cw),
            pl.BlockSpec((SUBLANES, cw), lambda i: (jnp.maximum(i * per - 1, 0), 0)),
            pl.BlockSpec((SUBLANES, cw), lambda i: (jnp.minimum((i + 1) * per, nhalo - 1), 0)),
            _const_spec(w_conv.shape),
        ],
        out_specs=_row_spec(rows.tm, cw),
        out_shape=jax.ShapeDtypeStruct((nblk * rows.tm, cw), BF16),
        compiler_params=_params("parallel"),
        name="conv_gate",
    )(gate_b, y, y, y, w_conv)


def _attn_kernel(q_ref, k_ref, v_ref, *rest, scale, sub):
    o_ref = rest[-1]
    k = k_ref[...]
    v = v_ref[...]
    for r0 in range(0, q_ref.shape[0], sub):
        s = lax.dot_general(q_ref[r0:r0 + sub, :], k, (((1,), (1,)), ((), ())), preferred_element_type=F32) * scale
        p = jnp.exp(s - jnp.max(s, axis=-1, keepdims=True))
        denom = jnp.sum(p, axis=-1, keepdims=True)
        o_ref[r0:r0 + sub, :] = (_dot(p.astype(BF16), v) / denom).astype(BF16)


def _attention(q, k, v, bsz, nheads, lq, tq, q_block0, lk, k_block_stride, out_rows, into=None):
    nq = lq // tq
    in_specs = [
        pl.BlockSpec((tq, HEAD_W), lambda b, hh, qi: (q_block0 + b * nq + qi, hh)),
        pl.BlockSpec((lk, HEAD_W), lambda b, hh, qi: (b * k_block_stride, hh)),
        pl.BlockSpec((lk, V_DIM), lambda b, hh, qi: (b * k_block_stride, hh)),
    ]
    args = [q, k, v]
    aliases = {}
    if into is not None:
        in_specs.append(pl.BlockSpec(memory_space=pl.ANY))
        args.append(into)
        aliases = {3: 0}
    return pl.pallas_call(
        functools.partial(_attn_kernel, scale=float(NOPE_DIM + ROPE_DIM) ** -0.5, sub=min(tq, CFG["tq_sub"])),
        grid=(bsz, nheads, nq),
        in_specs=in_specs,
        out_specs=pl.BlockSpec((tq, V_DIM), lambda b, hh, qi: (q_block0 + b * nq + qi, hh)),
        out_shape=jax.ShapeDtypeStruct((out_rows, nheads * V_DIM), BF16),
        input_output_aliases=aliases,
        compiler_params=_params("parallel", "parallel", "parallel"),
        name="attention",
    )(*args)


def _merge_kernel(h_ref, a_ref, gm_ref, cv_ref, wg0_ref, wg1_ref, wg2_ref, woa_ref, wob_ref, woc_ref, o_ref):
    h = h_ref[...]
    m = jax.nn.sigmoid(_dot(h, wg0_ref[...])) * _dot(a_ref[...], woa_ref[...])
    m = m + jax.nn.sigmoid(_dot(h, wg1_ref[...])) * _dot(gm_ref[...], wob_ref[...])
    m = m + jax.nn.sigmoid(_dot(h, wg2_ref[...])) * _dot(cv_ref[...], woc_ref[...])
    o_ref[...] = m.astype(BF16)


def _merge(nrows, lay, h, attn, gm, cv, w_all, w_oa, w_ob, w_oc):
    d = h.shape[1]
    tm, tn = CFG["tm_merge"], CFG["tn_merge"]
    ncol = d // tn
    gate0 = lay.block_index("gate")

    def act(width):
        return pl.BlockSpec((tm, width), lambda n, i: (i, 0))

    def gate_w(k):
        return pl.BlockSpec((d, tn), lambda n, i: (0, gate0 + k * ncol + n))

    def out_w(width):
        return pl.BlockSpec((width, tn), lambda n, i: (0, n))

    return pl.pallas_call(
        _merge_kernel,
        grid=(ncol, nrows // tm),
        in_specs=[act(d), act(attn.shape[1]), act(gm.shape[1]), act(cv.shape[1]),
                  gate_w(0), gate_w(1), gate_w(2),
                  out_w(w_oa.shape[0]), out_w(w_ob.shape[0]), out_w(w_oc.shape[0])],
        out_specs=pl.BlockSpec((tm, tn), lambda n, i: (i, n)),
        out_shape=jax.ShapeDtypeStruct((nrows, d), BF16),
        compiler_params=_params("parallel", "parallel"),
        name="merge_branches",
    )(h, attn, gm, cv, w_all, w_all, w_all, w_oa, w_ob, w_oc)


def _route(logits, carry, ne, ng):
    tm = logits.shape[0]
    neg = -1e30
    lane = lax.broadcasted_iota(jnp.int32, logits.shape, 1).astype(F32)
    big = float(4 * META_W)

    def first_lane(mask):
        return jnp.min(jnp.where(mask, lane, big), axis=-1, keepdims=True)

    gl = jnp.where((lane >= ne) & (lane < ne + ng), logits, neg)
    gmax = jnp.max(gl, axis=-1, keepdims=True)
    g_w = 1.0 / jnp.sum(jnp.exp(gl - gmax), axis=-1, keepdims=True)
    g_idx = first_lane(gl == gmax) - ne
    epg = ne // ng
    in_group = (lane >= g_idx * epg) & (lane < (g_idx + 1) * epg)
    el = jnp.where(in_group, logits, neg)
    ee = jnp.where(in_group, jnp.exp(el - jnp.max(el, axis=-1, keepdims=True)), -1.0)
    v1 = jnp.max(ee, axis=-1, keepdims=True)
    i1 = first_lane(ee == v1)
    ee2 = jnp.where(lane == i1, -1.0, ee)
    v2 = jnp.max(ee2, axis=-1, keepdims=True)
    i2 = first_lane(ee2 == v2)
    w1 = g_w * v1 / (v1 + v2)
    w2 = g_w * v2 / (v1 + v2)

    hit1 = lane == i1
    hit2 = lane == i2
    onehot = jnp.where(hit1 | hit2, 1.0, 0.0)
    r_i = lax.broadcasted_iota(jnp.int32, (tm, tm), 0)
    c_i = lax.broadcasted_iota(jnp.int32, (tm, tm), 1)
    earlier = jnp.where(r_i > c_i, 1.0, 0.0).astype(BF16)
    base = carry + _dot(earlier, onehot.astype(BF16))
    rank1 = jnp.sum(jnp.where(hit1, base, 0.0), axis=-1, keepdims=True)
    rank2 = jnp.sum(jnp.where(hit2, base, 0.0), axis=-1, keepdims=True)
    new_carry = carry + jnp.sum(onehot, axis=0, keepdims=True)

    rec = jnp.zeros(logits.shape, F32)
    for k, val in enumerate((i1, i2, rank1, rank2, w1, w2)):
        rec = jnp.where(lane == float(k), val, rec)
    return rec, new_carry


def _post1_kernel(mg_ref, xa_ref, xb_ref, wo_ref, gate_ref, sh_ref, sc_ref, lg_ref, lb_ref, wr_ref, br_ref,
                  x1_ref, t_ref, meta_ref, cnt_ref, carry_ref, *, nlat, alpha, ne, ng):
    i = pl.program_id(0)

    @pl.when(i == 0)
    def _():
        carry_ref[...] = jnp.zeros(carry_ref.shape, F32)

    x = jnp.where(i < nlat, xa_ref[...], xb_ref[...])
    y = _dot(mg_ref[...], wo_ref[...])
    x1 = _ln(alpha * x + gate_ref[...] * y) * lg_ref[...] + lb_ref[...]
    x1_ref[...] = x1
    t = _ln(x1) * (1.0 + sc_ref[...]) + sh_ref[...]
    t_ref[...] = _pack_rows(t)
    logits = _dot(t.astype(BF16), wr_ref[...]) + br_ref[...]
    rec, carry = _route(logits, carry_ref[0:1, :], ne, ng)
    meta_ref[...] = rec
    carry_ref[...] = jnp.broadcast_to(carry, carry_ref.shape)
    cnt_ref[...] = jnp.broadcast_to(carry, cnt_ref.shape)


def _post_mixer(rows, nblk, merged, xa, xb, w_o, m3, ln_g, ln_b, w_r, b_r, alpha, ne, ng):
    d = merged.shape[1]
    tm = rows.tm
    nrows = nblk * tm
    pw = _packed_width(d)
    return pl.pallas_call(
        functools.partial(_post1_kernel, nlat=rows.nlat, alpha=alpha, ne=ne, ng=ng),
        grid=(nblk,),
        in_specs=[_row_spec(tm, d)] + rows.two_source_specs(d) + [
            _const_spec((d, d)),
            rows.mod_spec(d, 2), rows.mod_spec(d, 3), rows.mod_spec(d, 4),
            _const_spec((1, d)), _const_spec((1, d)),
            _const_spec((d, META_W)), _const_spec((1, META_W)),
        ],
        out_specs=[_row_spec(tm, d), _row_spec(tm, pw), _row_spec(tm, META_W), _const_spec((SUBLANES, META_W))],
        out_shape=[
            jax.ShapeDtypeStruct((nrows, d), F32),
            jax.ShapeDtypeStruct((nrows, pw), jnp.uint32),
            jax.ShapeDtypeStruct((nrows, META_W), F32),
            jax.ShapeDtypeStruct((SUBLANES, META_W), F32),
        ],
        scratch_shapes=[pltpu.VMEM((SUBLANES, META_W), F32)],
        compiler_params=_params("arbitrary"),
        name="post_mixer_router",
    )(merged, xa, xb, w_o, m3, m3, m3, ln_g, ln_b, w_r, b_r)


ISSUE_UNROLL = 8


def _dispatch_kernel(dest_ref, tail_ref, t_ref, buf_ref, zero_ref, sem, zsem, *, tm, tme, ne):
    i = pl.program_id(0)

    @pl.when(i == 0)
    def _():
        zero_ref[...] = jnp.zeros(zero_ref.shape, zero_ref.dtype)

        def zero_copy(e):
            row0 = pl.multiple_of(jnp.maximum(tail_ref[e], 0) * tme, tme)
            return pltpu.make_async_copy(zero_ref, buf_ref.at[pl.ds(row0, tme)], zsem)

        for e in range(ne):
            @pl.when(tail_ref[e] >= 0)
            def _():
                zero_copy(e).start()
        for e in range(ne):
            @pl.when(tail_ref[e] >= 0)
            def _():
                zero_copy(e).wait()

    def start(r, carry):
        a = (i * tm + r) * 2
        for k in range(2):
            pltpu.make_async_copy(t_ref.at[pl.ds(r, 1)], buf_ref.at[pl.ds(dest_ref[a + k], 1)], sem).start()
        return carry

    lax.fori_loop(0, tm, start, 0, unroll=ISSUE_UNROLL)
    for k in range(2):
        pltpu.make_async_copy(t_ref, buf_ref.at[pl.ds(0, tm)], sem).wait()


def _dispatch(dest, tail_blk, t_packed, buf_rows, tm, tme):
    nrows, pw = t_packed.shape
    return pl.pallas_call(
        functools.partial(_dispatch_kernel, tm=tm, tme=tme, ne=tail_blk.shape[0]),
        grid_spec=pltpu.PrefetchScalarGridSpec(
            num_scalar_prefetch=2,
            grid=(nrows // tm,),
            in_specs=[pl.BlockSpec((tm, pw), lambda i, dest, tail: (i, 0))],
            out_specs=pl.BlockSpec(memory_space=pl.ANY),
            scratch_shapes=[pltpu.VMEM((tme, pw), t_packed.dtype),
                            pltpu.SemaphoreType.DMA(()), pltpu.SemaphoreType.DMA(())],
        ),
        out_shape=jax.ShapeDtypeStruct((buf_rows, pw), t_packed.dtype),
        compiler_params=_params("arbitrary"),
        name="moe_dispatch",
    )(dest, tail_blk, t_packed)


def _expert_kernel(be_ref, nu_ref, nxt_ref, x_ref, w1_hbm, w3_hbm, w2_hbm, y_ref,
                   s1_ref, s3_ref, s2_ref, b1_ref, b3_ref, b2_ref, sem, *, layer):
    i = pl.program_id(0)

    def fetch(e):
        return (pltpu.make_async_copy(w1_hbm.at[layer, e], s1_ref, sem.at[0]),
                pltpu.make_async_copy(w3_hbm.at[layer, e], s3_ref, sem.at[1]),
                pltpu.make_async_copy(w2_hbm.at[layer, e], s2_ref, sem.at[2]))

    @pl.when(i < nu_ref[0])
    def _():
        e = be_ref[i]
        run_start = jnp.logical_or(i == 0, e != be_ref[jnp.maximum(i - 1, 0)])

        @pl.when(i == 0)
        def _():
            for cp in fetch(e):
                cp.start()

        @pl.when(run_start)
        def _():
            for cp in fetch(e):
                cp.wait()
            b1_ref[...] = s1_ref[...].astype(BF16)
            b3_ref[...] = s3_ref[...].astype(BF16)
            b2_ref[...] = s2_ref[...].astype(BF16)
            nxt = nxt_ref[i]

            @pl.when(nxt >= 0)
            def _():
                for cp in fetch(nxt):
                    cp.start()

        lo, hi = _unpack_rows(x_ref[...])
        half = lo.shape[1]

        def proj(b_ref):
            return _dot(lo, b_ref[:half, :]) + _dot(hi, b_ref[half:, :])

        a = proj(b1_ref)
        hid = (a * jax.nn.sigmoid(a) * proj(b3_ref)).astype(BF16)
        y_ref[...] = _dot(hid, b2_ref[...])


def _experts(block_e, n_used, next_e, buf, w1, w3, w2, layer):
    tme = CFG["tme"]
    nb = buf.shape[0] // tme
    pw = buf.shape[1]
    _, _, d, eh = w1.shape

    def row_block(i, be, nu, nxt):
        return (jnp.minimum(i, nu[0] - 1), 0)

    hbm = pl.BlockSpec(memory_space=pl.ANY)
    return pl.pallas_call(
        functools.partial(_expert_kernel, layer=layer),
        grid_spec=pltpu.PrefetchScalarGridSpec(
            num_scalar_prefetch=3,
            grid=(nb,),
            in_specs=[pl.BlockSpec((tme, pw), row_block), hbm, hbm, hbm],
            out_specs=pl.BlockSpec((tme, d), row_block),
            scratch_shapes=[pltpu.VMEM((d, eh), F32), pltpu.VMEM((d, eh), F32), pltpu.VMEM((eh, d), F32),
                            pltpu.VMEM((d, eh), BF16), pltpu.VMEM((d, eh), BF16), pltpu.VMEM((eh, d), BF16),
                            pltpu.SemaphoreType.DMA((3,))],
        ),
        out_shape=jax.ShapeDtypeStruct((nb * tme, d), F32),
        compiler_params=_params("arbitrary"),
        name="experts",
    )(block_e, n_used, next_e, buf, w1, w3, w2)


def _combine_kernel(dest_ref, y_ref, meta_ref, x1_ref, gate_ref, lg_ref, lb_ref, sh_ref, sc_ref, *rest,
                    tm, alpha, emit_h, chunk):
    if emit_h:
        x2_ref, h_ref, gbuf, sem = rest
    else:
        x2_ref, gbuf, sem = rest
    i = pl.program_id(0)
    nsteps = pl.num_programs(0)

    def start_row(blk, slot, r):
        a = (blk * tm + r) * 2
        for k in range(2):
            pltpu.make_async_copy(y_ref.at[pl.ds(dest_ref[a + k], 1)], gbuf.at[slot, k, pl.ds(r, 1)],
                                  sem.at[slot]).start()

    def wait_slot(slot):
        for k in range(2):
            pltpu.make_async_copy(y_ref.at[pl.ds(0, tm)], gbuf.at[slot, k], sem.at[slot]).wait()

    @pl.when(i == 0)
    def _():
        def body(r, carry):
            start_row(0, 0, r)
            return carry
        lax.fori_loop(0, tm, body, 0, unroll=ISSUE_UNROLL)

    slot = i % 2
    wait_slot(slot)

    nxt = jnp.minimum(i + 1, nsteps - 1)
    for r0 in range(0, tm, chunk):
        for r in range(r0, r0 + chunk):
            start_row(nxt, 1 - slot, r)
        rs = slice(r0, r0 + chunk)
        meta = meta_ref[rs, :]
        f = meta[:, 4:5] * gbuf[slot, 0, rs, :] + meta[:, 5:6] * gbuf[slot, 1, rs, :]
        x2 = _ln(alpha * x1_ref[rs, :] + gate_ref[...] * f) * lg_ref[...] + lb_ref[...]
        x2_ref[rs, :] = x2
        if emit_h:
            h_ref[rs, :] = (_ln(x2) * (1.0 + sc_ref[...]) + sh_ref[...]).astype(BF16)

    @pl.when(i == nsteps - 1)
    def _():
        wait_slot(1 - slot)


def _combine(rows, nblk, dest, ybuf, meta, x1, m3, ln_g, ln_b, m3_next, alpha):
    d = x1.shape[1]
    tm = rows.tm
    nrows = nblk * tm
    emit_h = m3_next is not None
    if not emit_h:
        m3_next = m3

    def rspec(width):
        return pl.BlockSpec((tm, width), lambda i, dest: (i, 0))

    def cspec(shape):
        return pl.BlockSpec(shape, lambda i, dest: (0,) * len(shape))

    def mspec(seg):
        return pl.BlockSpec((None, 1, d), lambda i, dest: (rows.mod_row(i), 0, seg))

    out_specs = [rspec(d)]
    out_shape = [jax.ShapeDtypeStruct((nrows, d), F32)]
    if emit_h:
        out_specs.append(rspec(d))
        out_shape.append(jax.ShapeDtypeStruct((nrows, d), BF16))
    out = pl.pallas_call(
        functools.partial(_combine_kernel, tm=tm, alpha=alpha, emit_h=emit_h, chunk=min(tm, CFG["combine_chunk"])),
        grid_spec=pltpu.PrefetchScalarGridSpec(
            num_scalar_prefetch=1,
            grid=(nblk,),
            in_specs=[pl.BlockSpec(memory_space=pl.ANY), rspec(META_W), rspec(d), mspec(5),
                      cspec((1, d)), cspec((1, d)), mspec(0), mspec(1)],
            out_specs=out_specs,
            scratch_shapes=[pltpu.VMEM((2, 2, tm, d), F32), pltpu.SemaphoreType.DMA((2,))],
        ),
        out_shape=out_shape,
        compiler_params=_params("arbitrary"),
        name="moe_combine",
    )(dest, ybuf, meta, x1, m3, ln_g, ln_b, m3_next, m3_next)
    return out if emit_h else (out[0], None)


def _rot_cols(w):
    q = ROPE_DIM // 4
    return jnp.concatenate([-w[..., q:2 * q], w[..., 0:q], -w[..., 3 * q:4 * q], w[..., 2 * q:3 * q]], axis=-1)


def _reorder_w_in(wl, lay, ql, kvl, gwidth, cwidth):
    d = wl.shape[0]
    off_kr = ql + kvl
    off_gm = off_kr + ROPE_DIM
    off_cv = off_gm + 2 * gwidth
    off_gate = off_cv + 3 * cwidth
    kr = wl[:, off_kr:off_gm]
    pieces = {
        "qa": wl[:, :ql], "ckv": wl[:, ql:off_kr], "gm": wl[:, off_gm:off_cv],
        "cb": wl[:, off_cv:off_cv + cwidth], "cc": wl[:, off_cv + cwidth:off_cv + 2 * cwidth],
        "cx": wl[:, off_cv + 2 * cwidth:off_gate], "gate": wl[:, off_gate:],
        "kr": jnp.concatenate([kr, _rot_cols(kr)], axis=1),
    }
    cols, pos = [], 0
    for name, (off, width, _) in lay.segs.items():
        if off > pos:
            cols.append(jnp.zeros((d, off - pos), wl.dtype))
        cols.append(pieces[name])
        pos = off + width
    if lay.width > pos:
        cols.append(jnp.zeros((d, lay.width - pos), wl.dtype))
    return jnp.concatenate(cols, axis=1).astype(BF16)


def _rope_table(seq, tm):
    n_rows = seq // GRID_W
    t = jnp.arange(n_rows * GRID_W)
    row = (t // GRID_W).astype(F32)
    col = (t % GRID_W).astype(F32)
    half = ROPE_DIM // 2
    inv = ROPE_BASE ** (-jnp.arange(0, half, 2, dtype=F32) / half)
    ang = jnp.concatenate([row[:, None] * inv] * 2 + [col[:, None] * inv] * 2, axis=-1)
    cs = jnp.concatenate([jnp.cos(ang), jnp.sin(ang)], axis=-1)
    ident = jnp.concatenate([jnp.ones((tm, ROPE_DIM), F32), jnp.zeros((tm, ROPE_DIM), F32)], axis=-1)
    return jnp.concatenate([cs, ident], axis=0)


def kernel(x, c, ctx, c_ctx, w_mod, b_mod, w_in, g_q, w_uq, g_kv, w_ukv, w_oa, gm_ln_g, gm_ln_b, w_s, b_s, w_ob,
           w_conv, w_oc, w_o, ln1_g, ln1_b, w_group, b_group, w_expert, b_expert, w1, w3, w2, ln2_g, ln2_b):
    bsz, seq, d = x.shape
    ctx_len = ctx.shape[1]
    depth = w_mod.shape[0]
    ql = g_q.shape[1]
    kvl = g_kv.shape[1]
    nheads = w_uq.shape[2] // (NOPE_DIM + ROPE_DIM)
    gwidth = gm_ln_g.shape[1]
    cwidth = w_conv.shape[2]
    ne = w_expert.shape[2]
    ng = w_group.shape[2]
    alpha = float((2 * depth) ** 0.25)
    tm, tq, tme = CFG["tm"], CFG["tq"], CFG["tme"]
    rows = _Rows(bsz, seq, ctx_len, tm)
    lay = _InLayout(d, ql, kvl, gwidth, cwidth, CFG["tn_merge"])
    n_lat, n_ctx = bsz * seq, bsz * ctx_len

    mod_rows = -(-(bsz + 1) // SUBLANES) * SUBLANES
    cc = jnp.zeros((mod_rows, d), F32).at[:bsz].set(c).at[bsz].set(c_ctx)
    m_all = _mod_vectors(cc, w_mod, b_mod)
    cs = _rope_table(seq, tm)

    xa = x.reshape(n_lat, d)
    xb = ctx.reshape(n_ctx, d)
    h = None
    for l in range(depth):
        last = l == depth - 1
        m3 = m_all[l].reshape(mod_rows, 1, 6 * d)
        nblk = rows.nlat if last else rows.nall
        nrows = nblk * tm

        w_all = _reorder_w_in(w_in[l], lay, ql, kvl, gwidth, cwidth)
        uq = w_uq[l].reshape(ql, nheads, NOPE_DIM + ROPE_DIM)
        uq_rope = uq[..., NOPE_DIM:]
        w_uq_l = jnp.concatenate([uq, _rot_cols(uq_rope)], axis=-1).reshape(ql, nheads * HEAD_W).astype(BF16)
        ukv = w_ukv[l].reshape(kvl, nheads, NOPE_DIM + V_DIM)
        w_uk = ukv[..., :NOPE_DIM].reshape(kvl, nheads * NOPE_DIM).astype(BF16)
        w_uv = ukv[..., NOPE_DIM:].reshape(kvl, nheads * V_DIM).astype(BF16)
        bs_exp = jnp.broadcast_to(b_s[l][:, :, None], b_s.shape[1:] + (gwidth // w_s.shape[1],))
        w_r = jnp.zeros((d, META_W), F32).at[:, :ne].set(w_expert[l]).at[:, ne:ne + ng].set(w_group[l]).astype(BF16)
        b_r = jnp.zeros((1, META_W), F32).at[0, :ne].set(b_expert[l]).at[0, ne:ne + ng].set(b_group[l])

        if l == 0:
            h = _ln_modulate(rows, xa, xb, m3)

        q, kcat, vcat, gm, gate_b, y = _mixer_in(rows, nblk, lay, h, w_all, g_q[l][None], w_uq_l, g_kv[l][None],
                                                 w_uk, w_uv, gm_ln_g[l][None], gm_ln_b[l][None],
                                                 w_s[l].astype(BF16), bs_exp, cs, nheads)
        cv = _conv_gate(rows, nblk, gate_b, y, w_conv[l])
        lk = ctx_len + seq
        attn = _attention(q, kcat, vcat, bsz, nheads, seq, min(tq, seq), 0, lk, 1, nrows)
        if not last:
            assert lk % ctx_len == 0
            tqc = min(tq, ctx_len)
            attn = _attention(q, kcat, vcat, bsz, nheads, ctx_len, tqc, n_lat // tqc, ctx_len, lk // ctx_len, nrows,
                              into=attn)
        merged = _merge(nrows, lay, h, attn, gm, cv, w_all, w_oa[l].astype(BF16), w_ob[l].astype(BF16),
                        w_oc[l].astype(BF16))
        x1, t_packed, meta, cnt = _post_mixer(rows, nblk, merged, xa, xb, w_o[l].astype(BF16), m3, ln1_g[l][None],
                                              ln1_b[l][None], w_r, b_r, alpha, ne, ng)

        e_ids = meta[:, 0:2].astype(jnp.int32)
        ranks = meta[:, 2:4].astype(jnp.int32)
        counts = cnt[0, :ne].astype(jnp.int32)
        experts = jnp.arange(ne, dtype=jnp.int32)
        padded = (counts + tme - 1) // tme * tme
        pend = jnp.cumsum(padded)
        pstart = pend - padded
        dest = (jnp.sum(jnp.where(e_ids[..., None] == experts, pstart, 0), axis=-1) + ranks).reshape(-1)
        nb = 2 * nrows // tme + ne
        n_used = pend[-1:] // tme
        blk = jnp.minimum(jnp.arange(nb, dtype=jnp.int32), n_used[0] - 1)
        block_e = jnp.minimum(jnp.sum(pend[None, :] <= (blk * tme)[:, None], axis=1), ne - 1).astype(jnp.int32)
        tail_blk = jnp.where(padded > 0, pend // tme - 1, -1).astype(jnp.int32)
        nonempty_from = lax.cummin(jnp.where(padded > 0, experts, ne), axis=0, reverse=True)
        next_nonempty = jnp.concatenate([nonempty_from[1:], jnp.full((1,), ne, jnp.int32)])
        next_e = next_nonempty[block_e]
        next_e = jnp.where(next_e >= ne, -1, next_e).astype(jnp.int32)

        buf = _dispatch(dest, tail_blk, t_packed, nb * tme, tm, tme)
        ybuf = _experts(block_e, n_used.astype(jnp.int32), next_e, buf, w1, w3, w2, l)
        m3_next = None if last else m_all[l + 1].reshape(mod_rows, 1, 6 * d)
        x2, h = _combine(rows, nblk, dest, ybuf, meta, x1, m3, ln2_g[l][None], ln2_b[l][None], m3_next, alpha)
        xa, xb = x2, x2
    return xa.reshape(bsz, seq, d)
```

```python
import functools

import jax
import jax.numpy as jnp
from jax import lax
from jax.experimental import pallas as pl
from jax.experimental.pallas import tpu as pltpu

F32 = jnp.float32
BF16 = jnp.bfloat16

GRID_W = 64
NOPE_DIM = 128
ROPE_DIM = 64
V_DIM = 128
ROPE_BASE = 10000.0
LN_EPS = 1e-6
HEAD_W = NOPE_DIM + 2 * ROPE_DIM
META_W = 128
LANES = 128
SUBLANES = 8
LOG2_E = 1.4426950408889634

CFG = dict(
    tm=256,
    tq=2048,
    tq_sub=256,
    tm_merge=512,
    tn_merge=512,
    tn_mod=1024,
    tme=256,
    combine_chunk=32,
)
VMEM_LIMIT = 56 * 1024 * 1024


def _params(*sem):
    return pltpu.CompilerParams(dimension_semantics=sem, vmem_limit_bytes=VMEM_LIMIT)


def _dot(a, b):
    return jnp.dot(a, b, preferred_element_type=F32)


def _ln(x):
    mu = jnp.mean(x, axis=-1, keepdims=True)
    xc = x - mu
    var = jnp.mean(xc * xc, axis=-1, keepdims=True)
    return xc * lax.rsqrt(var + LN_EPS)


def _rms(x):
    return x * lax.rsqrt(jnp.mean(x * x, axis=-1, keepdims=True) + LN_EPS)


def _pack_rows(t):
    half = t.shape[1] // 2
    return pltpu.pack_elementwise([t[:, :half], t[:, half:]], packed_dtype=BF16)


def _unpack_rows(xp):
    lo = pltpu.unpack_elementwise(xp, index=0, packed_dtype=BF16, unpacked_dtype=F32)
    hi = pltpu.unpack_elementwise(xp, index=1, packed_dtype=BF16, unpacked_dtype=F32)
    return lo.astype(BF16), hi.astype(BF16)


def _packed_width(d):
    return d // 2


def _mod_kernel(c_ref, w_ref, b_ref, o_ref):
    c = c_ref[...]
    a = (c * jax.nn.sigmoid(c)).astype(BF16)
    o_ref[...] = _dot(a, w_ref[...].astype(BF16)) + b_ref[...]


def _mod_vectors(cc, w_mod, b_mod):
    nl, d, d6 = w_mod.shape
    rows = cc.shape[0]
    tn = CFG["tn_mod"]
    return pl.pallas_call(
        _mod_kernel,
        grid=(nl, d6 // tn),
        in_specs=[
            pl.BlockSpec((rows, d), lambda l, n: (0, 0)),
            pl.BlockSpec((None, d, tn), lambda l, n: (l, 0, n)),
            pl.BlockSpec((None, 1, tn), lambda l, n: (l, 0, n)),
        ],
        out_specs=pl.BlockSpec((None, rows, tn), lambda l, n: (l, 0, n)),
        out_shape=jax.ShapeDtypeStruct((nl, rows, d6), F32),
        compiler_params=_params("parallel", "parallel"),
        name="mod_vectors",
    )(cc, w_mod, b_mod.reshape(nl, 1, d6))


class _Rows:
    def __init__(self, bsz, seq, ctx_len, tm):
        assert seq % tm == 0 and ctx_len % tm == 0
        self.bsz, self.seq, self.ctx_len, self.tm = bsz, seq, ctx_len, tm
        self.nsb = seq // tm
        self.ncb = ctx_len // tm
        self.nlat = bsz * self.nsb
        self.nctx = bsz * self.ncb
        self.nall = self.nlat + self.nctx

    def mod_row(self, i):
        return jnp.where(i < self.nlat, i // self.nsb, self.bsz)

    def mod_spec(self, d, seg):
        return pl.BlockSpec((None, 1, d), lambda i: (self.mod_row(i), 0, seg))

    def rope_spec(self):
        return pl.BlockSpec((self.tm, 2 * ROPE_DIM), lambda i: (jnp.where(i < self.nlat, i % self.nsb, self.nsb), 0))

    def two_source_specs(self, d):
        return [
            pl.BlockSpec((self.tm, d), lambda i: (jnp.minimum(i, self.nlat - 1), 0)),
            pl.BlockSpec((self.tm, d), lambda i: (jnp.maximum(i - self.nlat, 0), 0)),
        ]

    def seq_edges(self, i):
        j = i - self.nlat
        first = jnp.where(i < self.nlat, i % self.nsb == 0, j % self.ncb == 0)
        last = jnp.where(i < self.nlat, i % self.nsb == self.nsb - 1, j % self.ncb == self.ncb - 1)
        return first, last


def _row_spec(tm, width):
    return pl.BlockSpec((tm, width), lambda i: (i, 0))


def _const_spec(shape):
    return pl.BlockSpec(shape, lambda i: (0,) * len(shape), pipeline_mode=pl.Buffered(1))


def _lnmod_kernel(xa_ref, xb_ref, sh_ref, sc_ref, h_ref, *, nlat):
    x = jnp.where(pl.program_id(0) < nlat, xa_ref[...], xb_ref[...])
    h_ref[...] = (_ln(x) * (1.0 + sc_ref[...]) + sh_ref[...]).astype(BF16)


def _ln_modulate(rows, xa, xb, m3):
    d = xa.shape[1]
    return pl.pallas_call(
        functools.partial(_lnmod_kernel, nlat=rows.nlat),
        grid=(rows.nall,),
        in_specs=rows.two_source_specs(d) + [rows.mod_spec(d, 0), rows.mod_spec(d, 1)],
        out_specs=_row_spec(rows.tm, d),
        out_shape=jax.ShapeDtypeStruct((rows.nall * rows.tm, d), BF16),
        compiler_params=_params("parallel"),
        name="ln_modulate",
    )(xa, xb, m3, m3)


class _InLayout:
    def __init__(self, d, ql, kvl, gwidth, cwidth, tn_gate):
        self.segs = {}
        off = 0
        for name, width, block in (("qa", ql, ql), ("ckv", kvl, kvl), ("gm", 2 * gwidth, 2 * gwidth),
                                   ("cb", cwidth, cwidth), ("cc", cwidth, cwidth), ("cx", cwidth, cwidth),
                                   ("gate", 3 * d, tn_gate), ("kr", 2 * ROPE_DIM, 2 * ROPE_DIM)):
            off = -(-off // block) * block
            self.segs[name] = (off, width, block)
            off += width
        self.width = -(-off // LANES) * LANES

    def block_index(self, name):
        off, _, block = self.segs[name]
        return off // block

    def spec(self, d, name):
        _, width, block = self.segs[name]
        assert width == block
        idx = self.block_index(name)
        return pl.BlockSpec((d, width), lambda i: (0, idx), pipeline_mode=pl.Buffered(1))


def _rope(x2, cs):
    w = x2 * cs
    return w + pltpu.roll(w, ROPE_DIM, axis=1)


def _mixer_in_kernel(h_ref, wqa_ref, wckv_ref, wkr_ref, wgm_ref, wcb_ref, wcc_ref, wcx_ref,
                     gq_ref, wuq_ref, gkv_ref, wuk_ref, wuv_ref, lg_ref, lb_ref, ws_ref, bs_ref, cs_ref,
                     q_ref, k_ref, v_ref, gm_ref, gb_ref, y_ref, *, nheads, nfull, width, chunk, groups):
    h = h_ref[...]
    cs = cs_ref[...]

    c = (_rms(_dot(h, wckv_ref[...])) * gkv_ref[...]).astype(BF16)
    kr = _rope(_dot(h, wkr_ref[...]), cs)
    lane = lax.broadcasted_iota(jnp.int32, kr.shape, 1)
    kr = jnp.where(lane < ROPE_DIM, kr, 0.0).astype(BF16)
    kn = _dot(c, wuk_ref[...]).astype(BF16)
    v_ref[...] = _dot(c, wuv_ref[...]).astype(BF16)
    for hh in range(nheads):
        k_ref[:, hh * HEAD_W:hh * HEAD_W + NOPE_DIM] = kn[:, hh * NOPE_DIM:(hh + 1) * NOPE_DIM]
        k_ref[:, hh * HEAD_W + NOPE_DIM:(hh + 1) * HEAD_W] = kr

    def rest():
        a = _rms(_dot(h, wqa_ref[...])) * gq_ref[...]
        q = _dot(a.astype(BF16), wuq_ref[...])
        for hh in range(nheads):
            base = hh * HEAD_W
            q_ref[:, base:base + NOPE_DIM] = q[:, base:base + NOPE_DIM].astype(BF16)
            q_ref[:, base + NOPE_DIM:base + HEAD_W] = _rope(q[:, base + NOPE_DIM:base + HEAD_W], cs).astype(BF16)
        z = jax.nn.gelu(_dot(h, wgm_ref[...]))
        u = z[:, :width]
        vv = (_ln(z[:, width:]) * lg_ref[...] + lb_ref[...]).astype(BF16)
        gw = width // groups
        for ci in range(z.shape[0] // chunk):
            r0 = ci * chunk
            for g in range(groups):
                c0 = g * gw
                mixed = _dot(ws_ref[g], vv[r0:r0 + chunk, c0:c0 + gw]) + bs_ref[g]
                gm_ref[r0:r0 + chunk, c0:c0 + gw] = (u[r0:r0 + chunk, c0:c0 + gw] * mixed).astype(BF16)
        gb_ref[...] = _dot(h, wcb_ref[...])
        y_ref[...] = _dot(h, wcc_ref[...]) * _dot(h, wcx_ref[...])

    if nfull is None:
        rest()
    else:
        pl.when(pl.program_id(0) < nfull)(rest)


def _mixer_in(rows, nblk, lay, h, w_all, g_q, w_uq, g_kv, w_uk, w_uv, ln_g, ln_b, w_s, bs_exp, cs, nheads):
    d = h.shape[1]
    tm = rows.tm
    ql = lay.segs["qa"][1]
    kvl = lay.segs["ckv"][1]
    width = lay.segs["gm"][1] // 2
    cw = lay.segs["cb"][1]
    groups, chunk, _ = w_s.shape
    assert tm % chunk == 0
    per_batch = rows.ncb + rows.nsb
    nrows = nblk * tm

    def kv_block(i):
        j = i - rows.nlat
        lat = (i // rows.nsb) * per_batch + rows.ncb + i % rows.nsb
        ctx = (j // rows.ncb) * per_batch + j % rows.ncb
        return jnp.where(i < rows.nlat, lat, ctx)

    def full_spec(w):
        return pl.BlockSpec((tm, w), lambda i: (jnp.minimum(i, nblk - 1), 0))

    def kv_spec(w):
        return pl.BlockSpec((tm, w), lambda i: (kv_block(i), 0))

    return pl.pallas_call(
        functools.partial(_mixer_in_kernel, nheads=nheads, nfull=None if nblk == rows.nall else nblk,
                          width=width, chunk=chunk, groups=groups),
        grid=(rows.nall,),
        in_specs=[_row_spec(tm, d)]
        + [lay.spec(d, name) for name in ("qa", "ckv", "kr", "gm", "cb", "cc", "cx")]
        + [_const_spec((1, ql)), _const_spec((ql, nheads * HEAD_W)),
           _const_spec((1, kvl)), _const_spec((kvl, nheads * NOPE_DIM)), _const_spec((kvl, nheads * V_DIM)),
           _const_spec((1, width)), _const_spec((1, width)),
           _const_spec((groups, chunk, chunk)), _const_spec((groups, chunk, width // groups)),
           rows.rope_spec()],
        out_specs=[full_spec(nheads * HEAD_W), kv_spec(nheads * HEAD_W), kv_spec(nheads * V_DIM),
                   full_spec(width), full_spec(cw), full_spec(cw)],
        out_shape=[
            jax.ShapeDtypeStruct((nrows, nheads * HEAD_W), BF16),
            jax.ShapeDtypeStruct((rows.nall * tm, nheads * HEAD_W), BF16),
            jax.ShapeDtypeStruct((rows.nall * tm, nheads * V_DIM), BF16),
            jax.ShapeDtypeStruct((nrows, width), BF16),
            jax.ShapeDtypeStruct((nrows, cw), F32),
            jax.ShapeDtypeStruct((nrows, cw), F32),
        ],
        compiler_params=_params("arbitrary"),
        name="mixer_in",
    )(h, *([w_all] * 7), g_q, w_uq, g_kv, w_uk, w_uv, ln_g, ln_b, w_s, bs_exp, cs)


def _conv_gate_kernel(gb_ref, y_ref, yp_ref, yn_ref, wc_ref, o_ref, *, rows):
    first, last = rows.seq_edges(pl.program_id(0))
    y = y_ref[...]
    y_prev = jnp.where(first, 0.0, yp_ref[SUBLANES - 1:SUBLANES, :])
    y_next = jnp.where(last, 0.0, yn_ref[0:1, :])
    tm = y.shape[0]
    row = lax.broadcasted_iota(jnp.int32, y.shape, 0)
    y_dn = jnp.where(row == 0, y_prev, pltpu.roll(y, 1, axis=0))
    y_up = jnp.where(row == tm - 1, y_next, pltpu.roll(y, tm - 1, axis=0))
    wc = wc_ref[...]
    conv = y_dn * wc[0:1, :] + y * wc[1:2, :] + y_up * wc[2:3, :]
    o_ref[...] = (gb_ref[...] * conv).astype(BF16)


def _conv_gate(rows, nblk, gate_b, y, w_conv):
    cw = y.shape[1]
    per = rows.tm // SUBLANES
    nhalo = y.shape[0] // SUBLANES
    return pl.pallas_call(
        functools.partial(_conv_gate_kernel, rows=rows),
        grid=(nblk,),
        in_specs=[
            _row_spec(rows.tm, cw),
            _row_spec(rows.tm, cw),
            pl.BlockSpec((SUBLANES, cw), lambda i: (jnp.maximum(i * per - 1, 0), 0)),
            pl.BlockSpec((SUBLANES, cw), lambda i: (jnp.minimum((i + 1) * per, nhalo - 1), 0)),
            _const_spec(w_conv.shape),
        ],
        out_specs=_row_spec(rows.tm, cw),
        out_shape=jax.ShapeDtypeStruct((nblk * rows.tm, cw), BF16),
        compiler_params=_params("parallel"),
        name="conv_gate",
    )(gate_b, y, y, y, w_conv)


def _attn_kernel(q_ref, k_ref, v_ref, *rest, scale, sub):
    o_ref = rest[-1]
    k = k_ref[...]
    v = v_ref[...]
    for r0 in range(0, q_ref.shape[0], sub):
        s = lax.dot_general(q_ref[r0:r0 + sub, :], k, (((1,), (1,)), ((), ())), preferred_element_type=F32)
        p = jnp.exp2((s - jnp.max(s, axis=-1, keepdims=True)) * (scale * LOG2_E))
        denom = jnp.sum(p, axis=-1, keepdims=True)
        o_ref[r0:r0 + sub, :] = (_dot(p.astype(BF16), v) / denom).astype(BF16)


def _attention(q, k, v, bsz, nheads, lq, tq, q_block0, lk, k_block_stride, out_rows, into=None):
    nq = lq // tq
    in_specs = [
        pl.BlockSpec((tq, HEAD_W), lambda b, hh, qi: (q_block0 + b * nq + qi, hh)),
        pl.BlockSpec((lk, HEAD_W), lambda b, hh, qi: (b * k_block_stride, hh)),
        pl.BlockSpec((lk, V_DIM), lambda b, hh, qi: (b * k_block_stride, hh)),
    ]
    args = [q, k, v]
    aliases = {}
    if into is not None:
        in_specs.append(pl.BlockSpec(memory_space=pl.ANY))
        args.append(into)
        aliases = {3: 0}
    return pl.pallas_call(
        functools.partial(_attn_kernel, scale=float(NOPE_DIM + ROPE_DIM) ** -0.5, sub=min(tq, CFG["tq_sub"])),
        grid=(bsz, nheads, nq),
        in_specs=in_specs,
        out_specs=pl.BlockSpec((tq, V_DIM), lambda b, hh, qi: (q_block0 + b * nq + qi, hh)),
        out_shape=jax.ShapeDtypeStruct((out_rows, nheads * V_DIM), BF16),
        input_output_aliases=aliases,
        compiler_params=_params("parallel", "parallel", "parallel"),
        name="attention",
    )(*args)


def _merge_kernel(h_ref, a_ref, gm_ref, cv_ref, wg0_ref, wg1_ref, wg2_ref, woa_ref, wob_ref, woc_ref, o_ref):
    h = h_ref[...]
    m = jax.nn.sigmoid(_dot(h, wg0_ref[...])) * _dot(a_ref[...], woa_ref[...])
    m = m + jax.nn.sigmoid(_dot(h, wg1_ref[...])) * _dot(gm_ref[...], wob_ref[...])
    m = m + jax.nn.sigmoid(_dot(h, wg2_ref[...])) * _dot(cv_ref[...], woc_ref[...])
    o_ref[...] = m.astype(BF16)


def _merge(nrows, lay, h, attn, gm, cv, w_all, w_oa, w_ob, w_oc):
    d = h.shape[1]
    tm, tn = CFG["tm_merge"], CFG["tn_merge"]
    ncol = d // tn
    gate0 = lay.block_index("gate")

    def act(width):
        return pl.BlockSpec((tm, width), lambda n, i: (i, 0))

    def gate_w(k):
        return pl.BlockSpec((d, tn), lambda n, i: (0, gate0 + k * ncol + n))

    def out_w(width):
        return pl.BlockSpec((width, tn), lambda n, i: (0, n))

    return pl.pallas_call(
        _merge_kernel,
        grid=(ncol, nrows // tm),
        in_specs=[act(d), act(attn.shape[1]), act(gm.shape[1]), act(cv.shape[1]),
                  gate_w(0), gate_w(1), gate_w(2),
                  out_w(w_oa.shape[0]), out_w(w_ob.shape[0]), out_w(w_oc.shape[0])],
        out_specs=pl.BlockSpec((tm, tn), lambda n, i: (i, n)),
        out_shape=jax.ShapeDtypeStruct((nrows, d), BF16),
        compiler_params=_params("parallel", "parallel"),
        name="merge_branches",
    )(h, attn, gm, cv, w_all, w_all, w_all, w_oa, w_ob, w_oc)


def _route(logits, carry, ne, ng):
    tm = logits.shape[0]
    neg = -1e30
    lane = lax.broadcasted_iota(jnp.int32, logits.shape, 1).astype(F32)
    big = float(4 * META_W)

    def first_lane(mask):
        return jnp.min(jnp.where(mask, lane, big), axis=-1, keepdims=True)

    gl = jnp.where((lane >= ne) & (lane < ne + ng), logits, neg)
    gmax = jnp.max(gl, axis=-1, keepdims=True)
    g_w = 1.0 / jnp.sum(jnp.exp(gl - gmax), axis=-1, keepdims=True)
    g_idx = first_lane(gl == gmax) - ne
    epg = ne // ng
    in_group = (lane >= g_idx * epg) & (lane < (g_idx + 1) * epg)
    el = jnp.where(in_group, logits, neg)
    ee = jnp.where(in_group, jnp.exp(el - jnp.max(el, axis=-1, keepdims=True)), -1.0)
    v1 = jnp.max(ee, axis=-1, keepdims=True)
    i1 = first_lane(ee == v1)
    ee2 = jnp.where(lane == i1, -1.0, ee)
    v2 = jnp.max(ee2, axis=-1, keepdims=True)
    i2 = first_lane(ee2 == v2)
    w1 = g_w * v1 / (v1 + v2)
    w2 = g_w * v2 / (v1 + v2)

    hit1 = lane == i1
    hit2 = lane == i2
    onehot = jnp.where(hit1 | hit2, 1.0, 0.0)
    r_i = lax.broadcasted_iota(jnp.int32, (tm, tm), 0)
    c_i = lax.broadcasted_iota(jnp.int32, (tm, tm), 1)
    earlier = jnp.where(r_i > c_i, 1.0, 0.0).astype(BF16)
    base = carry + _dot(earlier, onehot.astype(BF16))
    rank1 = jnp.sum(jnp.where(hit1, base, 0.0), axis=-1, keepdims=True)
    rank2 = jnp.sum(jnp.where(hit2, base, 0.0), axis=-1, keepdims=True)
    new_carry = carry + jnp.sum(onehot, axis=0, keepdims=True)

    rec = jnp.zeros(logits.shape, F32)
    for k, val in enumerate((i1, i2, rank1, rank2, w1, w2)):
        rec = jnp.where(lane == float(k), val, rec)
    return rec, new_carry


def _post1_kernel(mg_ref, xa_ref, xb_ref, wo_ref, gate_ref, sh_ref, sc_ref, lg_ref, lb_ref, wr_ref, br_ref,
                  x1_ref, t_ref, meta_ref, cnt_ref, carry_ref, *, nlat, alpha, ne, ng):
    i = pl.program_id(0)

    @pl.when(i == 0)
    def _():
        carry_ref[...] = jnp.zeros(carry_ref.shape, F32)

    x = jnp.where(i < nlat, xa_ref[...], xb_ref[...])
    y = _dot(mg_ref[...], wo_ref[...])
    x1 = _ln(alpha * x + gate_ref[...] * y) * lg_ref[...] + lb_ref[...]
    x1_ref[...] = x1
    t = _ln(x1) * (1.0 + sc_ref[...]) + sh_ref[...]
    t_ref[...] = _pack_rows(t)
    logits = _dot(t.astype(BF16), wr_ref[...]) + br_ref[...]
    rec, carry = _route(logits, carry_ref[0:1, :], ne, ng)
    meta_ref[...] = rec
    carry_ref[...] = jnp.broadcast_to(carry, carry_ref.shape)
    cnt_ref[...] = jnp.broadcast_to(carry, cnt_ref.shape)


def _post_mixer(rows, nblk, merged, xa, xb, w_o, m3, ln_g, ln_b, w_r, b_r, alpha, ne, ng):
    d = merged.shape[1]
    tm = rows.tm
    nrows = nblk * tm
    pw = _packed_width(d)
    return pl.pallas_call(
        functools.partial(_post1_kernel, nlat=rows.nlat, alpha=alpha, ne=ne, ng=ng),
        grid=(nblk,),
        in_specs=[_row_spec(tm, d)] + rows.two_source_specs(d) + [
            _const_spec((d, d)),
            rows.mod_spec(d, 2), rows.mod_spec(d, 3), rows.mod_spec(d, 4),
            _const_spec((1, d)), _const_spec((1, d)),
            _const_spec((d, META_W)), _const_spec((1, META_W)),
        ],
        out_specs=[_row_spec(tm, d), _row_spec(tm, pw), _row_spec(tm, META_W),
                   pl.BlockSpec((SUBLANES, META_W), lambda i: (0, 0))],
        out_shape=[
            jax.ShapeDtypeStruct((nrows, d), F32),
            jax.ShapeDtypeStruct((nrows, pw), jnp.uint32),
            jax.ShapeDtypeStruct((nrows, META_W), F32),
            jax.ShapeDtypeStruct((SUBLANES, META_W), F32),
        ],
        scratch_shapes=[pltpu.VMEM((SUBLANES, META_W), F32)],
        compiler_params=_params("arbitrary"),
        name="post_mixer_router",
    )(merged, xa, xb, w_o, m3, m3, m3, ln_g, ln_b, w_r, b_r)


ISSUE_UNROLL = 8


def _dispatch_kernel(dest_ref, tail_ref, t_ref, buf_ref, zero_ref, sem, zsem, *, tm, tme, ne):
    i = pl.program_id(0)

    @pl.when(i == 0)
    def _():
        zero_ref[...] = jnp.zeros(zero_ref.shape, zero_ref.dtype)

        def zero_copy(e):
            row0 = pl.multiple_of(jnp.maximum(tail_ref[e], 0) * tme, tme)
            return pltpu.make_async_copy(zero_ref, buf_ref.at[pl.ds(row0, tme)], zsem)

        for e in range(ne):
            @pl.when(tail_ref[e] >= 0)
            def _():
                zero_copy(e).start()
        for e in range(ne):
            @pl.when(tail_ref[e] >= 0)
            def _():
                zero_copy(e).wait()

    base = i * (2 * tm)
    for r in range(tm):
        for k in range(2):
            pltpu.make_async_copy(t_ref.at[pl.ds(r, 1)], buf_ref.at[pl.ds(dest_ref[base + 2 * r + k], 1)], sem).start()
    for k in range(2):
        pltpu.make_async_copy(t_ref, buf_ref.at[pl.ds(0, tm)], sem).wait()


def _dispatch(dest, tail_blk, t_packed, buf_rows, tm, tme):
    nrows, pw = t_packed.shape
    return pl.pallas_call(
        functools.partial(_dispatch_kernel, tm=tm, tme=tme, ne=tail_blk.shape[0]),
        grid_spec=pltpu.PrefetchScalarGridSpec(
            num_scalar_prefetch=2,
            grid=(nrows // tm,),
            in_specs=[pl.BlockSpec((tm, pw), lambda i, dest, tail: (i, 0))],
            out_specs=pl.BlockSpec(memory_space=pl.ANY),
            scratch_shapes=[pltpu.VMEM((tme, pw), t_packed.dtype),
                            pltpu.SemaphoreType.DMA(()), pltpu.SemaphoreType.DMA(())],
        ),
        out_shape=jax.ShapeDtypeStruct((buf_rows, pw), t_packed.dtype),
        compiler_params=_params("arbitrary"),
        name="moe_dispatch",
    )(dest, tail_blk, t_packed)


def _expert_kernel(be_ref, nu_ref, nxt_ref, x_ref, w1_hbm, w3_hbm, w2_hbm, y_ref,
                   s1_ref, s3_ref, s2_ref, b1_ref, b3_ref, b2_ref, sem, *, layer):
    i = pl.program_id(0)

    def fetch(e):
        return (pltpu.make_async_copy(w1_hbm.at[layer, e], s1_ref, sem.at[0]),
                pltpu.make_async_copy(w3_hbm.at[layer, e], s3_ref, sem.at[1]),
                pltpu.make_async_copy(w2_hbm.at[layer, e], s2_ref, sem.at[2]))

    @pl.when(i < nu_ref[0])
    def _():
        e = be_ref[i]
        run_start = jnp.logical_or(i == 0, e != be_ref[jnp.maximum(i - 1, 0)])

        @pl.when(i == 0)
        def _():
            for cp in fetch(e):
                cp.start()

        @pl.when(run_start)
        def _():
            for cp in fetch(e):
                cp.wait()
            b1_ref[...] = s1_ref[...].astype(BF16)
            b3_ref[...] = s3_ref[...].astype(BF16)
            b2_ref[...] = s2_ref[...].astype(BF16)
            nxt = nxt_ref[i]

            @pl.when(nxt >= 0)
            def _():
                for cp in fetch(nxt):
                    cp.start()

        lo, hi = _unpack_rows(x_ref[...])
        half = lo.shape[1]

        def proj(b_ref):
            return _dot(lo, b_ref[:half, :]) + _dot(hi, b_ref[half:, :])

        a = proj(b1_ref)
        hid = (a * jax.nn.sigmoid(a) * proj(b3_ref)).astype(BF16)
        y_ref[...] = _dot(hid, b2_ref[...])


def _experts(block_e, n_used, next_e, buf, w1, w3, w2, layer):
    tme = CFG["tme"]
    nb = buf.shape[0] // tme
    pw = buf.shape[1]
    _, _, d, eh = w1.shape

    def row_block(i, be, nu, nxt):
        return (jnp.minimum(i, nu[0] - 1), 0)

    hbm = pl.BlockSpec(memory_space=pl.ANY)
    return pl.pallas_call(
        functools.partial(_expert_kernel, layer=layer),
        grid_spec=pltpu.PrefetchScalarGridSpec(
            num_scalar_prefetch=3,
            grid=(nb,),
            in_specs=[pl.BlockSpec((tme, pw), row_block), hbm, hbm, hbm],
            out_specs=pl.BlockSpec((tme, d), row_block),
            scratch_shapes=[pltpu.VMEM((d, eh), F32), pltpu.VMEM((d, eh), F32), pltpu.VMEM((eh, d), F32),
                            pltpu.VMEM((d, eh), BF16), pltpu.VMEM((d, eh), BF16), pltpu.VMEM((eh, d), BF16),
                            pltpu.SemaphoreType.DMA((3,))],
        ),
        out_shape=jax.ShapeDtypeStruct((nb * tme, d), F32),
        compiler_params=_params("arbitrary"),
        name="experts",
    )(block_e, n_used, next_e, buf, w1, w3, w2)


def _combine_kernel(dest_ref, y_ref, meta_ref, x1_ref, gate_ref, lg_ref, lb_ref, sh_ref, sc_ref, *rest,
                    tm, alpha, emit_h, chunk):
    if emit_h:
        x2_ref, h_ref, gbuf, sem = rest
    else:
        x2_ref, gbuf, sem = rest
    i = pl.program_id(0)
    nsteps = pl.num_programs(0)

    def start_row(blk, slot, r):
        a = (blk * tm + r) * 2
        for k in range(2):
            pltpu.make_async_copy(y_ref.at[pl.ds(dest_ref[a + k], 1)], gbuf.at[slot, k, pl.ds(r, 1)],
                                  sem.at[slot]).start()

    def wait_slot(slot):
        for k in range(2):
            pltpu.make_async_copy(y_ref.at[pl.ds(0, tm)], gbuf.at[slot, k], sem.at[slot]).wait()

    @pl.when(i == 0)
    def _():
        def body(r, carry):
            start_row(0, 0, r)
            return carry
        lax.fori_loop(0, tm, body, 0, unroll=ISSUE_UNROLL)

    slot = i % 2
    wait_slot(slot)

    nxt = jnp.minimum(i + 1, nsteps - 1)
    for r0 in range(0, tm, chunk):
        for r in range(r0, r0 + chunk):
            start_row(nxt, 1 - slot, r)
        rs = slice(r0, r0 + chunk)
        meta = meta_ref[rs, :]
        f = meta[:, 4:5] * gbuf[slot, 0, rs, :] + meta[:, 5:6] * gbuf[slot, 1, rs, :]
        x2 = _ln(alpha * x1_ref[rs, :] + gate_ref[...] * f) * lg_ref[...] + lb_ref[...]
        x2_ref[rs, :] = x2
        if emit_h:
            h_ref[rs, :] = (_ln(x2) * (1.0 + sc_ref[...]) + sh_ref[...]).astype(BF16)

    @pl.when(i == nsteps - 1)
    def _():
        wait_slot(1 - slot)


def _combine(rows, nblk, dest, ybuf, meta, x1, m3, ln_g, ln_b, m3_next, alpha):
    d = x1.shape[1]
    tm = rows.tm
    nrows = nblk * tm
    emit_h = m3_next is not None
    if not emit_h:
        m3_next = m3

    def rspec(width):
        return pl.BlockSpec((tm, width), lambda i, dest: (i, 0))

    def cspec(shape):
        return pl.BlockSpec(shape, lambda i, dest: (0,) * len(shape))

    def mspec(seg):
        return pl.BlockSpec((None, 1, d), lambda i, dest: (rows.mod_row(i), 0, seg))

    out_specs = [rspec(d)]
    out_shape = [jax.ShapeDtypeStruct((nrows, d), F32)]
    if emit_h:
        out_specs.append(rspec(d))
        out_shape.append(jax.ShapeDtypeStruct((nrows, d), BF16))
    out = pl.pallas_call(
        functools.partial(_combine_kernel, tm=tm, alpha=alpha, emit_h=emit_h, chunk=min(tm, CFG["combine_chunk"])),
        grid_spec=pltpu.PrefetchScalarGridSpec(
            num_scalar_prefetch=1,
            grid=(nblk,),
            in_specs=[pl.BlockSpec(memory_space=pl.ANY), rspec(META_W), rspec(d), mspec(5),
                      cspec((1, d)), cspec((1, d)), mspec(0), mspec(1)],
            out_specs=out_specs,
            scratch_shapes=[pltpu.VMEM((2, 2, tm, d), F32), pltpu.SemaphoreType.DMA((2,))],
        ),
        out_shape=out_shape,
        compiler_params=_params("arbitrary"),
        name="moe_combine",
    )(dest, ybuf, meta, x1, m3, ln_g, ln_b, m3_next, m3_next)
    return out if emit_h else (out[0], None)


def _rot_cols(w):
    q = ROPE_DIM // 4
    return jnp.concatenate([-w[..., q:2 * q], w[..., 0:q], -w[..., 3 * q:4 * q], w[..., 2 * q:3 * q]], axis=-1)


def _reorder_kernel(w_ref, o_ref, *, moves, zeros):
    for dst, width in zeros:
        o_ref[:, dst:dst + width] = jnp.zeros((o_ref.shape[0], width), BF16)
    for dst, src, width, negate in moves:
        v = w_ref[:, src:src + width]
        o_ref[:, dst:dst + width] = (-v if negate else v).astype(BF16)


def _reorder_w_in(w_in, layer, lay, ql, kvl, gwidth, cwidth):
    _, d, p_in = w_in.shape
    off_kr = ql + kvl
    off_gm = off_kr + ROPE_DIM
    off_cv = off_gm + 2 * gwidth
    off_gate = off_cv + 3 * cwidth
    src_of = {"qa": 0, "ckv": ql, "gm": off_gm, "cb": off_cv, "cc": off_cv + cwidth, "cx": off_cv + 2 * cwidth,
              "gate": off_gate}
    moves, zeros, pos = [], [], 0
    for name, (off, width, _) in lay.segs.items():
        if off > pos:
            zeros.append((pos, off - pos))
        if name == "kr":
            q = ROPE_DIM // 4
            moves.append((off, off_kr, ROPE_DIM, False))
            for j, (src_q, negate) in enumerate(((1, True), (0, False), (3, True), (2, False))):
                moves.append((off + ROPE_DIM + j * q, off_kr + src_q * q, q, negate))
        else:
            moves.append((off, src_of[name], width, False))
        pos = off + width
    if lay.width > pos:
        zeros.append((pos, lay.width - pos))
    tr = min(d, CFG["tm"])
    return pl.pallas_call(
        functools.partial(_reorder_kernel, moves=tuple(moves), zeros=tuple(zeros)),
        grid=(d // tr,),
        in_specs=[pl.BlockSpec((None, tr, p_in), lambda i: (layer, i, 0))],
        out_specs=pl.BlockSpec((tr, lay.width), lambda i: (i, 0)),
        out_shape=jax.ShapeDtypeStruct((d, lay.width), BF16),
        compiler_params=_params("parallel"),
        name="reorder_w_in",
    )(w_in)


def _rope_table(seq, tm):
    n_rows = seq // GRID_W
    t = jnp.arange(n_rows * GRID_W)
    row = (t // GRID_W).astype(F32)
    col = (t % GRID_W).astype(F32)
    half = ROPE_DIM // 2
    inv = ROPE_BASE ** (-jnp.arange(0, half, 2, dtype=F32) / half)
    ang = jnp.concatenate([row[:, None] * inv] * 2 + [col[:, None] * inv] * 2, axis=-1)
    cs = jnp.concatenate([jnp.cos(ang), jnp.sin(ang)], axis=-1)
    ident = jnp.concatenate([jnp.ones((tm, ROPE_DIM), F32), jnp.zeros((tm, ROPE_DIM), F32)], axis=-1)
    return jnp.concatenate([cs, ident], axis=0)


def kernel(x, c, ctx, c_ctx, w_mod, b_mod, w_in, g_q, w_uq, g_kv, w_ukv, w_oa, gm_ln_g, gm_ln_b, w_s, b_s, w_ob,
           w_conv, w_oc, w_o, ln1_g, ln1_b, w_group, b_group, w_expert, b_expert, w1, w3, w2, ln2_g, ln2_b):
    bsz, seq, d = x.shape
    ctx_len = ctx.shape[1]
    depth = w_mod.shape[0]
    ql = g_q.shape[1]
    kvl = g_kv.shape[1]
    nheads = w_uq.shape[2] // (NOPE_DIM + ROPE_DIM)
    gwidth = gm_ln_g.shape[1]
    cwidth = w_conv.shape[2]
    ne = w_expert.shape[2]
    ng = w_group.shape[2]
    alpha = float((2 * depth) ** 0.25)
    tm, tq, tme = CFG["tm"], CFG["tq"], CFG["tme"]
    rows = _Rows(bsz, seq, ctx_len, tm)
    lay = _InLayout(d, ql, kvl, gwidth, cwidth, CFG["tn_merge"])
    n_lat, n_ctx = bsz * seq, bsz * ctx_len

    mod_rows = -(-(bsz + 1) // SUBLANES) * SUBLANES
    cc = jnp.zeros((mod_rows, d), F32).at[:bsz].set(c).at[bsz].set(c_ctx)
    m_all = _mod_vectors(cc, w_mod, b_mod)
    cs = _rope_table(seq, tm)

    xa = x.reshape(n_lat, d)
    xb = ctx.reshape(n_ctx, d)
    h = None
    for l in range(depth):
        last = l == depth - 1
        m3 = m_all[l].reshape(mod_rows, 1, 6 * d)
        nblk = rows.nlat if last else rows.nall
        nrows = nblk * tm

        w_all = _reorder_w_in(w_in, l, lay, ql, kvl, gwidth, cwidth)
        uq = w_uq[l].reshape(ql, nheads, NOPE_DIM + ROPE_DIM)
        uq_rope = uq[..., NOPE_DIM:]
        w_uq_l = jnp.concatenate([uq, _rot_cols(uq_rope)], axis=-1).reshape(ql, nheads * HEAD_W).astype(BF16)
        ukv = w_ukv[l].reshape(kvl, nheads, NOPE_DIM + V_DIM)
        w_uk = ukv[..., :NOPE_DIM].reshape(kvl, nheads * NOPE_DIM).astype(BF16)
        w_uv = ukv[..., NOPE_DIM:].reshape(kvl, nheads * V_DIM).astype(BF16)
        bs_exp = jnp.broadcast_to(b_s[l][:, :, None], b_s.shape[1:] + (gwidth // w_s.shape[1],))
        w_r = jnp.zeros((d, META_W), F32).at[:, :ne].set(w_expert[l]).at[:, ne:ne + ng].set(w_group[l]).astype(BF16)
        b_r = jnp.zeros((1, META_W), F32).at[0, :ne].set(b_expert[l]).at[0, ne:ne + ng].set(b_group[l])

        if l == 0:
            h = _ln_modulate(rows, xa, xb, m3)

        q, kcat, vcat, gm, gate_b, y = _mixer_in(rows, nblk, lay, h, w_all, g_q[l][None], w_uq_l, g_kv[l][None],
                                                 w_uk, w_uv, gm_ln_g[l][None], gm_ln_b[l][None],
                                                 w_s[l].astype(BF16), bs_exp, cs, nheads)
        cv = _conv_gate(rows, nblk, gate_b, y, w_conv[l])
        lk = ctx_len + seq
        attn = _attention(q, kcat, vcat, bsz, nheads, seq, min(tq, seq), 0, lk, 1, nrows)
        if not last:
            assert lk % ctx_len == 0
            tqc = min(tq, ctx_len)
            attn = _attention(q, kcat, vcat, bsz, nheads, ctx_len, tqc, n_lat // tqc, ctx_len, lk // ctx_len, nrows,
                              into=attn)
        merged = _merge(nrows, lay, h, attn, gm, cv, w_all, w_oa[l].astype(BF16), w_ob[l].astype(BF16),
                        w_oc[l].astype(BF16))
        x1, t_packed, meta, cnt = _post_mixer(rows, nblk, merged, xa, xb, w_o[l].astype(BF16), m3, ln1_g[l][None],
                                              ln1_b[l][None], w_r, b_r, alpha, ne, ng)

        e_ids = meta[:, 0:2].astype(jnp.int32)
        ranks = meta[:, 2:4].astype(jnp.int32)
        counts = cnt[0, :ne].astype(jnp.int32)
        experts = jnp.arange(ne, dtype=jnp.int32)
        padded = (counts + tme - 1) // tme * tme
        pend = jnp.cumsum(padded)
        pstart = pend - padded
        dest = (jnp.sum(jnp.where(e_ids[..., None] == experts, pstart, 0), axis=-1) + ranks).reshape(-1)
        nb = 2 * nrows // tme + ne
        n_used = pend[-1:] // tme
        blk = jnp.minimum(jnp.arange(nb, dtype=jnp.int32), n_used[0] - 1)
        block_e = jnp.minimum(jnp.sum(pend[None, :] <= (blk * tme)[:, None], axis=1), ne - 1).astype(jnp.int32)
        tail_blk = jnp.where(padded > 0, pend // tme - 1, -1).astype(jnp.int32)
        nonempty_from = lax.cummin(jnp.where(padded > 0, experts, ne), axis=0, reverse=True)
        next_nonempty = jnp.concatenate([nonempty_from[1:], jnp.full((1,), ne, jnp.int32)])
        next_e = next_nonempty[block_e]
        next_e = jnp.where(next_e >= ne, -1, next_e).astype(jnp.int32)

        buf = _dispatch(dest, tail_blk, t_packed, nb * tme, tm, tme)
        ybuf = _experts(block_e, n_used.astype(jnp.int32), next_e, buf, w1, w3, w2, l)
        m3_next = None if last else m_all[l + 1].reshape(mod_rows, 1, 6 * d)
        x2, h = _combine(rows, nblk, dest, ybuf, meta, x1, m3, ln2_g[l][None], ln2_b[l][None], m3_next, alpha)
        xa, xb = x2, x2
    return xa.reshape(bsz, seq, d)
```

```python
import functools

import jax
import jax.numpy as jnp
from jax import lax
from jax.experimental import pallas as pl
from jax.experimental.pallas import tpu as pltpu

F32 = jnp.float32
BF16 = jnp.bfloat16

GRID_W = 64
NOPE_DIM = 128
ROPE_DIM = 64
V_DIM = 128
ROPE_BASE = 10000.0
LN_EPS = 1e-6
HEAD_W = NOPE_DIM + 2 * ROPE_DIM
META_W = 128
LANES = 128
SUBLANES = 8
LOG2_E = 1.4426950408889634

CFG = dict(
    tm=256,
    tq=2048,
    tq_sub=256,
    tm_merge=512,
    tn_merge=512,
    tn_mod=1024,
    tme=256,
    combine_chunk=32,
)
VMEM_LIMIT = 56 * 1024 * 1024


def _params(*sem):
    return pltpu.CompilerParams(dimension_semantics=sem, vmem_limit_bytes=VMEM_LIMIT)


def _dot(a, b):
    return jnp.dot(a, b, preferred_element_type=F32)


def _ln(x):
    mu = jnp.mean(x, axis=-1, keepdims=True)
    xc = x - mu
    var = jnp.mean(xc * xc, axis=-1, keepdims=True)
    return xc * lax.rsqrt(var + LN_EPS)


def _rms(x):
    return x * lax.rsqrt(jnp.mean(x * x, axis=-1, keepdims=True) + LN_EPS)


def _pack_rows(t):
    half = t.shape[1] // 2
    return pltpu.pack_elementwise([t[:, :half], t[:, half:]], packed_dtype=BF16)


def _unpack_rows(xp):
    lo = pltpu.unpack_elementwise(xp, index=0, packed_dtype=BF16, unpacked_dtype=F32)
    hi = pltpu.unpack_elementwise(xp, index=1, packed_dtype=BF16, unpacked_dtype=F32)
    return lo.astype(BF16), hi.astype(BF16)


def _packed_width(d):
    return d // 2


def _mod_kernel(c_ref, w_ref, b_ref, o_ref):
    c = c_ref[...]
    a = (c * jax.nn.sigmoid(c)).astype(BF16)
    o_ref[...] = _dot(a, w_ref[...].astype(BF16)) + b_ref[...]


def _mod_vectors(cc, w_mod, b_mod):
    nl, d, d6 = w_mod.shape
    rows = cc.shape[0]
    tn = CFG["tn_mod"]
    return pl.pallas_call(
        _mod_kernel,
        grid=(nl, d6 // tn),
        in_specs=[
            pl.BlockSpec((rows, d), lambda l, n: (0, 0)),
            pl.BlockSpec((None, d, tn), lambda l, n: (l, 0, n)),
            pl.BlockSpec((None, 1, tn), lambda l, n: (l, 0, n)),
        ],
        out_specs=pl.BlockSpec((None, rows, tn), lambda l, n: (l, 0, n)),
        out_shape=jax.ShapeDtypeStruct((nl, rows, d6), F32),
        compiler_params=_params("parallel", "parallel"),
        name="mod_vectors",
    )(cc, w_mod, b_mod.reshape(nl, 1, d6))


class _Rows:
    def __init__(self, bsz, seq, ctx_len, tm):
        assert seq % tm == 0 and ctx_len % tm == 0
        self.bsz, self.seq, self.ctx_len, self.tm = bsz, seq, ctx_len, tm
        self.nsb = seq // tm
        self.ncb = ctx_len // tm
        self.nlat = bsz * self.nsb
        self.nctx = bsz * self.ncb
        self.nall = self.nlat + self.nctx

    def mod_row(self, i):
        return jnp.where(i < self.nlat, i // self.nsb, self.bsz)

    def mod_spec(self, d, seg):
        return pl.BlockSpec((None, 1, d), lambda i: (self.mod_row(i), 0, seg))

    def rope_spec(self):
        return pl.BlockSpec((self.tm, 2 * ROPE_DIM), lambda i: (jnp.where(i < self.nlat, i % self.nsb, self.nsb), 0))

    def two_source_specs(self, d):
        return [
            pl.BlockSpec((self.tm, d), lambda i: (jnp.minimum(i, self.nlat - 1), 0)),
            pl.BlockSpec((self.tm, d), lambda i: (jnp.maximum(i - self.nlat, 0), 0)),
        ]

    def seq_edges(self, i):
        j = i - self.nlat
        first = jnp.where(i < self.nlat, i % self.nsb == 0, j % self.ncb == 0)
        last = jnp.where(i < self.nlat, i % self.nsb == self.nsb - 1, j % self.ncb == self.ncb - 1)
        return first, last


def _row_spec(tm, width):
    return pl.BlockSpec((tm, width), lambda i: (i, 0))


def _const_spec(shape):
    return pl.BlockSpec(shape, lambda i: (0,) * len(shape), pipeline_mode=pl.Buffered(1))


def _lnmod_kernel(xa_ref, xb_ref, sh_ref, sc_ref, h_ref, *, nlat):
    x = jnp.where(pl.program_id(0) < nlat, xa_ref[...], xb_ref[...])
    h_ref[...] = (_ln(x) * (1.0 + sc_ref[...]) + sh_ref[...]).astype(BF16)


def _ln_modulate(rows, xa, xb, m3):
    d = xa.shape[1]
    return pl.pallas_call(
        functools.partial(_lnmod_kernel, nlat=rows.nlat),
        grid=(rows.nall,),
        in_specs=rows.two_source_specs(d) + [rows.mod_spec(d, 0), rows.mod_spec(d, 1)],
        out_specs=_row_spec(rows.tm, d),
        out_shape=jax.ShapeDtypeStruct((rows.nall * rows.tm, d), BF16),
        compiler_params=_params("parallel"),
        name="ln_modulate",
    )(xa, xb, m3, m3)


class _InLayout:
    def __init__(self, d, ql, kvl, gwidth, cwidth, tn_gate):
        self.segs = {}
        off = 0
        for name, width, block in (("qa", ql, ql), ("ckv", kvl, kvl), ("gm", 2 * gwidth, 2 * gwidth),
                                   ("cb", cwidth, cwidth), ("cc", cwidth, cwidth), ("cx", cwidth, cwidth),
                                   ("gate", 3 * d, tn_gate), ("kr", 2 * ROPE_DIM, 2 * ROPE_DIM)):
            off = -(-off // block) * block
            self.segs[name] = (off, width, block)
            off += width
        self.width = -(-off // LANES) * LANES

    def block_index(self, name):
        off, _, block = self.segs[name]
        return off // block

    def spec(self, d, name):
        _, width, block = self.segs[name]
        assert width == block
        idx = self.block_index(name)
        return pl.BlockSpec((d, width), lambda i: (0, idx), pipeline_mode=pl.Buffered(1))


def _rope(x2, cs):
    w = x2 * cs
    return w + pltpu.roll(w, ROPE_DIM, axis=1)


def _mixer_in_kernel(h_ref, wqa_ref, wckv_ref, wkr_ref, wgm_ref, wcb_ref, wcc_ref, wcx_ref,
                     gq_ref, wuq_ref, gkv_ref, wuk_ref, wuv_ref, lg_ref, lb_ref, ws_ref, bs_ref, cs_ref,
                     q_ref, k_ref, v_ref, gm_ref, gb_ref, y_ref, *, nheads, nfull, width, chunk, groups):
    h = h_ref[...]
    cs = cs_ref[...]

    c = (_rms(_dot(h, wckv_ref[...])) * gkv_ref[...]).astype(BF16)
    kr = _rope(_dot(h, wkr_ref[...]), cs)
    lane = lax.broadcasted_iota(jnp.int32, kr.shape, 1)
    kr = jnp.where(lane < ROPE_DIM, kr, 0.0).astype(BF16)
    kn = _dot(c, wuk_ref[...]).astype(BF16)
    v_ref[...] = _dot(c, wuv_ref[...]).astype(BF16)
    for hh in range(nheads):
        k_ref[:, hh * HEAD_W:hh * HEAD_W + NOPE_DIM] = kn[:, hh * NOPE_DIM:(hh + 1) * NOPE_DIM]
        k_ref[:, hh * HEAD_W + NOPE_DIM:(hh + 1) * HEAD_W] = kr

    def rest():
        a = _rms(_dot(h, wqa_ref[...])) * gq_ref[...]
        q = _dot(a.astype(BF16), wuq_ref[...])
        for hh in range(nheads):
            base = hh * HEAD_W
            q_ref[:, base:base + NOPE_DIM] = q[:, base:base + NOPE_DIM].astype(BF16)
            q_ref[:, base + NOPE_DIM:base + HEAD_W] = _rope(q[:, base + NOPE_DIM:base + HEAD_W], cs).astype(BF16)
        z = jax.nn.gelu(_dot(h, wgm_ref[...]))
        u = z[:, :width]
        vv = (_ln(z[:, width:]) * lg_ref[...] + lb_ref[...]).astype(BF16)
        gw = width // groups
        for ci in range(z.shape[0] // chunk):
            r0 = ci * chunk
            for g in range(groups):
                c0 = g * gw
                mixed = _dot(ws_ref[g], vv[r0:r0 + chunk, c0:c0 + gw]) + bs_ref[g]
                gm_ref[r0:r0 + chunk, c0:c0 + gw] = (u[r0:r0 + chunk, c0:c0 + gw] * mixed).astype(BF16)
        gb_ref[...] = _dot(h, wcb_ref[...])
        y_ref[...] = _dot(h, wcc_ref[...]) * _dot(h, wcx_ref[...])

    if nfull is None:
        rest()
    else:
        pl.when(pl.program_id(0) < nfull)(rest)


def _mixer_in(rows, nblk, lay, h, w_all, g_q, w_uq, g_kv, w_uk, w_uv, ln_g, ln_b, w_s, bs_exp, cs, nheads):
    d = h.shape[1]
    tm = rows.tm
    ql = lay.segs["qa"][1]
    kvl = lay.segs["ckv"][1]
    width = lay.segs["gm"][1] // 2
    cw = lay.segs["cb"][1]
    groups, chunk, _ = w_s.shape
    assert tm % chunk == 0
    per_batch = rows.ncb + rows.nsb
    nrows = nblk * tm

    def kv_block(i):
        j = i - rows.nlat
        lat = (i // rows.nsb) * per_batch + rows.ncb + i % rows.nsb
        ctx = (j // rows.ncb) * per_batch + j % rows.ncb
        return jnp.where(i < rows.nlat, lat, ctx)

    def full_spec(w):
        return pl.BlockSpec((tm, w), lambda i: (jnp.minimum(i, nblk - 1), 0))

    def kv_spec(w):
        return pl.BlockSpec((tm, w), lambda i: (kv_block(i), 0))

    return pl.pallas_call(
        functools.partial(_mixer_in_kernel, nheads=nheads, nfull=None if nblk == rows.nall else nblk,
                          width=width, chunk=chunk, groups=groups),
        grid=(rows.nall,),
        in_specs=[_row_spec(tm, d)]
        + [lay.spec(d, name) for name in ("qa", "ckv", "kr", "gm", "cb", "cc", "cx")]
        + [_const_spec((1, ql)), _const_spec((ql, nheads * HEAD_W)),
           _const_spec((1, kvl)), _const_spec((kvl, nheads * NOPE_DIM)), _const_spec((kvl, nheads * V_DIM)),
           _const_spec((1, width)), _const_spec((1, width)),
           _const_spec((groups, chunk, chunk)), _const_spec((groups, chunk, width // groups)),
           rows.rope_spec()],
        out_specs=[full_spec(nheads * HEAD_W), kv_spec(nheads * HEAD_W), kv_spec(nheads * V_DIM),
                   full_spec(width), full_spec(cw), full_spec(cw)],
        out_shape=[
            jax.ShapeDtypeStruct((nrows, nheads * HEAD_W), BF16),
            jax.ShapeDtypeStruct((rows.nall * tm, nheads * HEAD_W), BF16),
            jax.ShapeDtypeStruct((rows.nall * tm, nheads * V_DIM), BF16),
            jax.ShapeDtypeStruct((nrows, width), BF16),
            jax.ShapeDtypeStruct((nrows, cw), F32),
            jax.ShapeDtypeStruct((nrows, cw), F32),
        ],
        compiler_params=_params("arbitrary"),
        name="mixer_in",
    )(h, *([w_all] * 7), g_q, w_uq, g_kv, w_uk, w_uv, ln_g, ln_b, w_s, bs_exp, cs)


def _conv_gate_kernel(gb_ref, y_ref, yp_ref, yn_ref, wc_ref, o_ref, *, rows):
    first, last = rows.seq_edges(pl.program_id(0))
    y = y_ref[...]
    y_prev = jnp.where(first, 0.0, yp_ref[SUBLANES - 1:SUBLANES, :])
    y_next = jnp.where(last, 0.0, yn_ref[0:1, :])
    tm = y.shape[0]
    row = lax.broadcasted_iota(jnp.int32, y.shape, 0)
    y_dn = jnp.where(row == 0, y_prev, pltpu.roll(y, 1, axis=0))
    y_up = jnp.where(row == tm - 1, y_next, pltpu.roll(y, tm - 1, axis=0))
    wc = wc_ref[...]
    conv = y_dn * wc[0:1, :] + y * wc[1:2, :] + y_up * wc[2:3, :]
    o_ref[...] = (gb_ref[...] * conv).astype(BF16)


def _conv_gate(rows, nblk, gate_b, y, w_conv):
    cw = y.shape[1]
    per = rows.tm // SUBLANES
    nhalo = y.shape[0] // SUBLANES
    return pl.pallas_call(
        functools.partial(_conv_gate_kernel, rows=rows),
        grid=(nblk,),
        in_specs=[
            _row_spec(rows.tm, cw),
            _row_spec(rows.tm, cw),
            pl.BlockSpec((SUBLANES, cw), lambda i: (jnp.maximum(i * per - 1, 0), 0)),
            pl.BlockSpec((SUBLANES, cw), lambda i: (jnp.minimum((i + 1) * per, nhalo - 1), 0)),
            _const_spec(w_conv.shape),
        ],
        out_specs=_row_spec(rows.tm, cw),
        out_shape=jax.ShapeDtypeStruct((nblk * rows.tm, cw), BF16),
        compiler_params=_params("parallel"),
        name="conv_gate",
    )(gate_b, y, y, y, w_conv)


def _attn_kernel(q_ref, k_ref, v_ref, *rest, scale, sub):
    o_ref = rest[-1]
    k = k_ref[...]
    v = v_ref[...]
    for r0 in range(0, q_ref.shape[0], sub):
        s = lax.dot_general(q_ref[r0:r0 + sub, :], k, (((1,), (1,)), ((), ())), preferred_element_type=F32)
        p = jnp.exp2((s - jnp.max(s, axis=-1, keepdims=True)) * (scale * LOG2_E))
        denom = jnp.sum(p, axis=-1, keepdims=True)
        o_ref[r0:r0 + sub, :] = (_dot(p.astype(BF16), v) / denom).astype(BF16)


def _attention(q, k, v, bsz, nheads, lq, tq, q_block0, lk, k_block_stride, out_rows, into=None):
    nq = lq // tq
    in_specs = [
        pl.BlockSpec((tq, HEAD_W), lambda b, hh, qi: (q_block0 + b * nq + qi, hh)),
        pl.BlockSpec((lk, HEAD_W), lambda b, hh, qi: (b * k_block_stride, hh)),
        pl.BlockSpec((lk, V_DIM), lambda b, hh, qi: (b * k_block_stride, hh)),
    ]
    args = [q, k, v]
    aliases = {}
    if into is not None:
        in_specs.append(pl.BlockSpec(memory_space=pl.ANY))
        args.append(into)
        aliases = {3: 0}
    return pl.pallas_call(
        functools.partial(_attn_kernel, scale=float(NOPE_DIM + ROPE_DIM) ** -0.5, sub=min(tq, CFG["tq_sub"])),
        grid=(bsz, nheads, nq),
        in_specs=in_specs,
        out_specs=pl.BlockSpec((tq, V_DIM), lambda b, hh, qi: (q_block0 + b * nq + qi, hh)),
        out_shape=jax.ShapeDtypeStruct((out_rows, nheads * V_DIM), BF16),
        input_output_aliases=aliases,
        compiler_params=_params("parallel", "parallel", "parallel"),
        name="attention",
    )(*args)


def _merge_kernel(h_ref, a_ref, gm_ref, cv_ref, wg0_ref, wg1_ref, wg2_ref, woa_ref, wob_ref, woc_ref, o_ref):
    h = h_ref[...]
    m = jax.nn.sigmoid(_dot(h, wg0_ref[...])) * _dot(a_ref[...], woa_ref[...])
    m = m + jax.nn.sigmoid(_dot(h, wg1_ref[...])) * _dot(gm_ref[...], wob_ref[...])
    m = m + jax.nn.sigmoid(_dot(h, wg2_ref[...])) * _dot(cv_ref[...], woc_ref[...])
    o_ref[...] = m.astype(BF16)


def _merge(nrows, lay, h, attn, gm, cv, w_all, w_oa, w_ob, w_oc):
    d = h.shape[1]
    tm, tn = CFG["tm_merge"], CFG["tn_merge"]
    ncol = d // tn
    gate0 = lay.block_index("gate")

    def act(width):
        return pl.BlockSpec((tm, width), lambda n, i: (i, 0))

    def gate_w(k):
        return pl.BlockSpec((d, tn), lambda n, i: (0, gate0 + k * ncol + n))

    def out_w(width):
        return pl.BlockSpec((width, tn), lambda n, i: (0, n))

    return pl.pallas_call(
        _merge_kernel,
        grid=(ncol, nrows // tm),
        in_specs=[act(d), act(attn.shape[1]), act(gm.shape[1]), act(cv.shape[1]),
                  gate_w(0), gate_w(1), gate_w(2),
                  out_w(w_oa.shape[0]), out_w(w_ob.shape[0]), out_w(w_oc.shape[0])],
        out_specs=pl.BlockSpec((tm, tn), lambda n, i: (i, n)),
        out_shape=jax.ShapeDtypeStruct((nrows, d), BF16),
        compiler_params=_params("parallel", "parallel"),
        name="merge_branches",
    )(h, attn, gm, cv, w_all, w_all, w_all, w_oa, w_ob, w_oc)


def _route(logits, carry, ne, ng):
    tm = logits.shape[0]
    neg = -1e30
    lane = lax.broadcasted_iota(jnp.int32, logits.shape, 1).astype(F32)
    big = float(4 * META_W)

    def first_lane(mask):
        return jnp.min(jnp.where(mask, lane, big), axis=-1, keepdims=True)

    gl = jnp.where((lane >= ne) & (lane < ne + ng), logits, neg)
    gmax = jnp.max(gl, axis=-1, keepdims=True)
    g_w = 1.0 / jnp.sum(jnp.exp(gl - gmax), axis=-1, keepdims=True)
    g_idx = first_lane(gl == gmax) - ne
    epg = ne // ng
    in_group = (lane >= g_idx * epg) & (lane < (g_idx + 1) * epg)
    el = jnp.where(in_group, logits, neg)
    ee = jnp.where(in_group, jnp.exp(el - jnp.max(el, axis=-1, keepdims=True)), -1.0)
    v1 = jnp.max(ee, axis=-1, keepdims=True)
    i1 = first_lane(ee == v1)
    ee2 = jnp.where(lane == i1, -1.0, ee)
    v2 = jnp.max(ee2, axis=-1, keepdims=True)
    i2 = first_lane(ee2 == v2)
    w1 = g_w * v1 / (v1 + v2)
    w2 = g_w * v2 / (v1 + v2)

    hit1 = lane == i1
    hit2 = lane == i2
    onehot = jnp.where(hit1 | hit2, 1.0, 0.0)
    r_i = lax.broadcasted_iota(jnp.int32, (tm, tm), 0)
    c_i = lax.broadcasted_iota(jnp.int32, (tm, tm), 1)
    earlier = jnp.where(r_i > c_i, 1.0, 0.0).astype(BF16)
    base = carry + _dot(earlier, onehot.astype(BF16))
    rank1 = jnp.sum(jnp.where(hit1, base, 0.0), axis=-1, keepdims=True)
    rank2 = jnp.sum(jnp.where(hit2, base, 0.0), axis=-1, keepdims=True)
    new_carry = carry + jnp.sum(onehot, axis=0, keepdims=True)

    rec = jnp.zeros(logits.shape, F32)
    for k, val in enumerate((i1, i2, rank1, rank2, w1, w2)):
        rec = jnp.where(lane == float(k), val, rec)
    return rec, new_carry


def _post1_kernel(mg_ref, xa_ref, xb_ref, wo_ref, gate_ref, sh_ref, sc_ref, lg_ref, lb_ref, wr_ref, br_ref,
                  x1_ref, t_ref, meta_ref, cnt_ref, carry_ref, *, nlat, alpha, ne, ng):
    i = pl.program_id(0)

    @pl.when(i == 0)
    def _():
        carry_ref[...] = jnp.zeros(carry_ref.shape, F32)

    x = jnp.where(i < nlat, xa_ref[...], xb_ref[...])
    y = _dot(mg_ref[...], wo_ref[...])
    x1 = _ln(alpha * x + gate_ref[...] * y) * lg_ref[...] + lb_ref[...]
    x1_ref[...] = x1
    t = _ln(x1) * (1.0 + sc_ref[...]) + sh_ref[...]
    t_ref[...] = _pack_rows(t)
    logits = _dot(t.astype(BF16), wr_ref[...]) + br_ref[...]
    rec, carry = _route(logits, carry_ref[0:1, :], ne, ng)
    meta_ref[...] = rec
    carry_ref[...] = jnp.broadcast_to(carry, carry_ref.shape)
    cnt_ref[...] = jnp.broadcast_to(carry, cnt_ref.shape)


def _post_mixer(rows, nblk, merged, xa, xb, w_o, m3, ln_g, ln_b, w_r, b_r, alpha, ne, ng):
    d = merged.shape[1]
    tm = rows.tm
    nrows = nblk * tm
    pw = _packed_width(d)
    return pl.pallas_call(
        functools.partial(_post1_kernel, nlat=rows.nlat, alpha=alpha, ne=ne, ng=ng),
        grid=(nblk,),
        in_specs=[_row_spec(tm, d)] + rows.two_source_specs(d) + [
            _const_spec((d, d)),
            rows.mod_spec(d, 2), rows.mod_spec(d, 3), rows.mod_spec(d, 4),
            _const_spec((1, d)), _const_spec((1, d)),
            _const_spec((d, META_W)), _const_spec((1, META_W)),
        ],
        out_specs=[_row_spec(tm, d), _row_spec(tm, pw), _row_spec(tm, META_W),
                   pl.BlockSpec((SUBLANES, META_W), lambda i: (0, 0))],
        out_shape=[
            jax.ShapeDtypeStruct((nrows, d), F32),
            jax.ShapeDtypeStruct((nrows, pw), jnp.uint32),
            jax.ShapeDtypeStruct((nrows, META_W), F32),
            jax.ShapeDtypeStruct((SUBLANES, META_W), F32),
        ],
        scratch_shapes=[pltpu.VMEM((SUBLANES, META_W), F32)],
        compiler_params=_params("arbitrary"),
        name="post_mixer_router",
    )(merged, xa, xb, w_o, m3, m3, m3, ln_g, ln_b, w_r, b_r)


ISSUE_UNROLL = 8


def _dispatch_kernel(dest_ref, tail_ref, t_ref, buf_ref, zero_ref, sem, zsem, *, tm, tme, ne):
    i = pl.program_id(0)

    @pl.when(i == 0)
    def _():
        zero_ref[...] = jnp.zeros(zero_ref.shape, zero_ref.dtype)

        def zero_copy(e):
            row0 = pl.multiple_of(jnp.maximum(tail_ref[e], 0) * tme, tme)
            return pltpu.make_async_copy(zero_ref, buf_ref.at[pl.ds(row0, tme)], zsem)

        for e in range(ne):
            @pl.when(tail_ref[e] >= 0)
            def _():
                zero_copy(e).start()
        for e in range(ne):
            @pl.when(tail_ref[e] >= 0)
            def _():
                zero_copy(e).wait()

    base = i * (2 * tm)
    for r in range(tm):
        for k in range(2):
            pltpu.make_async_copy(t_ref.at[pl.ds(r, 1)], buf_ref.at[pl.ds(dest_ref[base + 2 * r + k], 1)], sem).start()
    for k in range(2):
        pltpu.make_async_copy(t_ref, buf_ref.at[pl.ds(0, tm)], sem).wait()


def _dispatch(dest, tail_blk, t_packed, buf_rows, tm, tme):
    nrows, pw = t_packed.shape
    return pl.pallas_call(
        functools.partial(_dispatch_kernel, tm=tm, tme=tme, ne=tail_blk.shape[0]),
        grid_spec=pltpu.PrefetchScalarGridSpec(
            num_scalar_prefetch=2,
            grid=(nrows // tm,),
            in_specs=[pl.BlockSpec((tm, pw), lambda i, dest, tail: (i, 0))],
            out_specs=pl.BlockSpec(memory_space=pl.ANY),
            scratch_shapes=[pltpu.VMEM((tme, pw), t_packed.dtype),
                            pltpu.SemaphoreType.DMA(()), pltpu.SemaphoreType.DMA(())],
        ),
        out_shape=jax.ShapeDtypeStruct((buf_rows, pw), t_packed.dtype),
        compiler_params=_params("arbitrary"),
        name="moe_dispatch",
    )(dest, tail_blk, t_packed)


def _expert_kernel(be_ref, nu_ref, nxt_ref, x_ref, w1_hbm, w3_hbm, w2_hbm, y_ref,
                   s1_ref, s3_ref, s2_ref, b1_ref, b3_ref, b2_ref, sem, *, layer):
    i = pl.program_id(0)

    def fetch(e):
        return (pltpu.make_async_copy(w1_hbm.at[layer, e], s1_ref, sem.at[0]),
                pltpu.make_async_copy(w3_hbm.at[layer, e], s3_ref, sem.at[1]),
                pltpu.make_async_copy(w2_hbm.at[layer, e], s2_ref, sem.at[2]))

    @pl.when(i < nu_ref[0])
    def _():
        e = be_ref[i]
        run_start = jnp.logical_or(i == 0, e != be_ref[jnp.maximum(i - 1, 0)])

        @pl.when(i == 0)
        def _():
            for cp in fetch(e):
                cp.start()

        @pl.when(run_start)
        def _():
            for cp in fetch(e):
                cp.wait()
            b1_ref[...] = s1_ref[...].astype(BF16)
            b3_ref[...] = s3_ref[...].astype(BF16)
            b2_ref[...] = s2_ref[...].astype(BF16)
            nxt = nxt_ref[i]

            @pl.when(nxt >= 0)
            def _():
                for cp in fetch(nxt):
                    cp.start()

        lo, hi = _unpack_rows(x_ref[...])
        half = lo.shape[1]

        def proj(b_ref):
            return _dot(lo, b_ref[:half, :]) + _dot(hi, b_ref[half:, :])

        a = proj(b1_ref)
        hid = (a * jax.nn.sigmoid(a) * proj(b3_ref)).astype(BF16)
        y_ref[...] = _dot(hid, b2_ref[...])


def _experts(block_e, n_used, next_e, buf, w1, w3, w2, layer):
    tme = CFG["tme"]
    nb = buf.shape[0] // tme
    pw = buf.shape[1]
    _, _, d, eh = w1.shape

    def row_block(i, be, nu, nxt):
        return (jnp.minimum(i, nu[0] - 1), 0)

    hbm = pl.BlockSpec(memory_space=pl.ANY)
    return pl.pallas_call(
        functools.partial(_expert_kernel, layer=layer),
        grid_spec=pltpu.PrefetchScalarGridSpec(
            num_scalar_prefetch=3,
            grid=(nb,),
            in_specs=[pl.BlockSpec((tme, pw), row_block), hbm, hbm, hbm],
            out_specs=pl.BlockSpec((tme, d), row_block),
            scratch_shapes=[pltpu.VMEM((d, eh), F32), pltpu.VMEM((d, eh), F32), pltpu.VMEM((eh, d), F32),
                            pltpu.VMEM((d, eh), BF16), pltpu.VMEM((d, eh), BF16), pltpu.VMEM((eh, d), BF16),
                            pltpu.SemaphoreType.DMA((3,))],
        ),
        out_shape=jax.ShapeDtypeStruct((nb * tme, d), F32),
        compiler_params=_params("arbitrary"),
        name="experts",
    )(block_e, n_used, next_e, buf, w1, w3, w2)


def _combine_kernel(dest_ref, y_ref, meta_ref, x1_ref, gate_ref, lg_ref, lb_ref, sh_ref, sc_ref, *rest,
                    tm, alpha, emit_h, chunk):
    if emit_h:
        x2_ref, h_ref, gbuf, sem = rest
    else:
        x2_ref, gbuf, sem = rest
    i = pl.program_id(0)
    nsteps = pl.num_programs(0)

    def start_row(blk, slot, r):
        a = (blk * tm + r) * 2
        for k in range(2):
            pltpu.make_async_copy(y_ref.at[pl.ds(dest_ref[a + k], 1)], gbuf.at[slot, k, pl.ds(r, 1)],
                                  sem.at[slot]).start()

    def wait_slot(slot):
        for k in range(2):
            pltpu.make_async_copy(y_ref.at[pl.ds(0, tm)], gbuf.at[slot, k], sem.at[slot]).wait()

    @pl.when(i == 0)
    def _():
        def body(r, carry):
            start_row(0, 0, r)
            return carry
        lax.fori_loop(0, tm, body, 0, unroll=ISSUE_UNROLL)

    slot = i % 2
    wait_slot(slot)

    nxt = jnp.minimum(i + 1, nsteps - 1)
    for r0 in range(0, tm, chunk):
        for r in range(r0, r0 + chunk):
            start_row(nxt, 1 - slot, r)
        rs = slice(r0, r0 + chunk)
        meta = meta_ref[rs, :]
        f = meta[:, 4:5] * gbuf[slot, 0, rs, :] + meta[:, 5:6] * gbuf[slot, 1, rs, :]
        x2 = _ln(alpha * x1_ref[rs, :] + gate_ref[...] * f) * lg_ref[...] + lb_ref[...]
        x2_ref[rs, :] = x2
        if emit_h:
            h_ref[rs, :] = (_ln(x2) * (1.0 + sc_ref[...]) + sh_ref[...]).astype(BF16)

    @pl.when(i == nsteps - 1)
    def _():
        wait_slot(1 - slot)


def _combine(rows, nblk, dest, ybuf, meta, x1, m3, ln_g, ln_b, m3_next, alpha):
    d = x1.shape[1]
    tm = rows.tm
    nrows = nblk * tm
    emit_h = m3_next is not None
    if not emit_h:
        m3_next = m3

    def rspec(width):
        return pl.BlockSpec((tm, width), lambda i, dest: (i, 0))

    def cspec(shape):
        return pl.BlockSpec(shape, lambda i, dest: (0,) * len(shape))

    def mspec(seg):
        return pl.BlockSpec((None, 1, d), lambda i, dest: (rows.mod_row(i), 0, seg))

    out_specs = [rspec(d)]
    out_shape = [jax.ShapeDtypeStruct((nrows, d), F32)]
    if emit_h:
        out_specs.append(rspec(d))
        out_shape.append(jax.ShapeDtypeStruct((nrows, d), BF16))
    out = pl.pallas_call(
        functools.partial(_combine_kernel, tm=tm, alpha=alpha, emit_h=emit_h, chunk=min(tm, CFG["combine_chunk"])),
        grid_spec=pltpu.PrefetchScalarGridSpec(
            num_scalar_prefetch=1,
            grid=(nblk,),
            in_specs=[pl.BlockSpec(memory_space=pl.ANY), rspec(META_W), rspec(d), mspec(5),
                      cspec((1, d)), cspec((1, d)), mspec(0), mspec(1)],
            out_specs=out_specs,
            scratch_shapes=[pltpu.VMEM((2, 2, tm, d), F32), pltpu.SemaphoreType.DMA((2,))],
        ),
        out_shape=out_shape,
        compiler_params=_params("arbitrary"),
        name="moe_combine",
    )(dest, ybuf, meta, x1, m3, ln_g, ln_b, m3_next, m3_next)
    return out if emit_h else (out[0], None)


def _rot_cols(w):
    q = ROPE_DIM // 4
    return jnp.concatenate([-w[..., q:2 * q], w[..., 0:q], -w[..., 3 * q:4 * q], w[..., 2 * q:3 * q]], axis=-1)


KIND_COPY, KIND_ROPE, KIND_ZERO = 0, 1, 2


def _reorder_kernel(ia_ref, ib_ref, kind_ref, a_ref, b_ref, o_ref):
    del ia_ref, ib_ref
    kind = kind_ref[pl.program_id(0)]
    a = a_ref[...]

    @pl.when(kind == KIND_COPY)
    def _():
        o_ref[...] = jnp.concatenate([a, b_ref[...]], axis=0).T.astype(BF16)

    @pl.when(kind == KIND_ROPE)
    def _():
        q = ROPE_DIM // 4
        rot = jnp.concatenate([-a[q:2 * q], a[0:q], -a[3 * q:4 * q], a[2 * q:3 * q]], axis=0)
        o_ref[...] = jnp.concatenate([a, rot], axis=0).T.astype(BF16)

    @pl.when(kind == KIND_ZERO)
    def _():
        o_ref[...] = jnp.zeros(o_ref.shape, BF16)


def _reorder_w_in(w_in, layer, lay, ql, kvl, gwidth, cwidth):
    _, d, p_in = w_in.shape
    piece = ROPE_DIM
    off_kr = ql + kvl
    off_gm = off_kr + ROPE_DIM
    off_cv = off_gm + 2 * gwidth
    off_gate = off_cv + 3 * cwidth
    src_of = {"qa": 0, "ckv": ql, "gm": off_gm, "cb": off_cv, "cc": off_cv + cwidth, "cx": off_cv + 2 * cwidth,
              "gate": off_gate, "kr": off_kr}
    nblk = lay.width // LANES
    ia = [0] * nblk
    ib = [0] * nblk
    kind = [KIND_ZERO] * nblk
    for name, (off, width, _) in lay.segs.items():
        assert off % LANES == 0 and src_of[name] % piece == 0
        for j in range(width // LANES):
            src = src_of[name] + j * LANES
            ia[off // LANES + j] = src // piece
            ib[off // LANES + j] = src // piece if name == "kr" else src // piece + 1
            kind[off // LANES + j] = KIND_ROPE if name == "kr" else KIND_COPY
    tables = [jnp.asarray(t, dtype=jnp.int32) for t in (ia, ib, kind)]
    return pl.pallas_call(
        _reorder_kernel,
        grid_spec=pltpu.PrefetchScalarGridSpec(
            num_scalar_prefetch=3,
            grid=(nblk,),
            in_specs=[pl.BlockSpec((None, piece, d), lambda j, ia, ib, kind: (layer, ia[j], 0)),
                      pl.BlockSpec((None, piece, d), lambda j, ia, ib, kind: (layer, ib[j], 0))],
            out_specs=pl.BlockSpec((d, LANES), lambda j, ia, ib, kind: (0, j)),
        ),
        out_shape=jax.ShapeDtypeStruct((d, lay.width), BF16),
        compiler_params=_params("arbitrary"),
        name="reorder_w_in",
    )(*tables, *([jnp.swapaxes(w_in, 1, 2)] * 2))


def _rope_table(seq, tm):
    n_rows = seq // GRID_W
    t = jnp.arange(n_rows * GRID_W)
    row = (t // GRID_W).astype(F32)
    col = (t % GRID_W).astype(F32)
    half = ROPE_DIM // 2
    inv = ROPE_BASE ** (-jnp.arange(0, half, 2, dtype=F32) / half)
    ang = jnp.concatenate([row[:, None] * inv] * 2 + [col[:, None] * inv] * 2, axis=-1)
    cs = jnp.concatenate([jnp.cos(ang), jnp.sin(ang)], axis=-1)
    ident = jnp.concatenate([jnp.ones((tm, ROPE_DIM), F32), jnp.zeros((tm, ROPE_DIM), F32)], axis=-1)
    return jnp.concatenate([cs, ident], axis=0)


def kernel(x, c, ctx, c_ctx, w_mod, b_mod, w_in, g_q, w_uq, g_kv, w_ukv, w_oa, gm_ln_g, gm_ln_b, w_s, b_s, w_ob,
           w_conv, w_oc, w_o, ln1_g, ln1_b, w_group, b_group, w_expert, b_expert, w1, w3, w2, ln2_g, ln2_b):
    bsz, seq, d = x.shape
    ctx_len = ctx.shape[1]
    depth = w_mod.shape[0]
    ql = g_q.shape[1]
    kvl = g_kv.shape[1]
    nheads = w_uq.shape[2] // (NOPE_DIM + ROPE_DIM)
    gwidth = gm_ln_g.shape[1]
    cwidth = w_conv.shape[2]
    ne = w_expert.shape[2]
    ng = w_group.shape[2]
    alpha = float((2 * depth) ** 0.25)
    tm, tq, tme = CFG["tm"], CFG["tq"], CFG["tme"]
    rows = _Rows(bsz, seq, ctx_len, tm)
    lay = _InLayout(d, ql, kvl, gwidth, cwidth, CFG["tn_merge"])
    n_lat, n_ctx = bsz * seq, bsz * ctx_len

    mod_rows = -(-(bsz + 1) // SUBLANES) * SUBLANES
    cc = jnp.zeros((mod_rows, d), F32).at[:bsz].set(c).at[bsz].set(c_ctx)
    m_all = _mod_vectors(cc, w_mod, b_mod)
    cs = _rope_table(seq, tm)

    xa = x.reshape(n_lat, d)
    xb = ctx.reshape(n_ctx, d)
    h = None
    for l in range(depth):
        last = l == depth - 1
        m3 = m_all[l].reshape(mod_rows, 1, 6 * d)
        nblk = rows.nlat if last else rows.nall
        nrows = nblk * tm

        w_all = _reorder_w_in(w_in, l, lay, ql, kvl, gwidth, cwidth)
        uq = w_uq[l].reshape(ql, nheads, NOPE_DIM + ROPE_DIM)
        uq_rope = uq[..., NOPE_DIM:]
        w_uq_l = jnp.concatenate([uq, _rot_cols(uq_rope)], axis=-1).reshape(ql, nheads * HEAD_W).astype(BF16)
        ukv = w_ukv[l].reshape(kvl, nheads, NOPE_DIM + V_DIM)
        w_uk = ukv[..., :NOPE_DIM].reshape(kvl, nheads * NOPE_DIM).astype(BF16)
        w_uv = ukv[..., NOPE_DIM:].reshape(kvl, nheads * V_DIM).astype(BF16)
        bs_exp = jnp.broadcast_to(b_s[l][:, :, None], b_s.shape[1:] + (gwidth // w_s.shape[1],))
        w_r = jnp.zeros((d, META_W), F32).at[:, :ne].set(w_expert[l]).at[:, ne:ne + ng].set(w_group[l]).astype(BF16)
        b_r = jnp.zeros((1, META_W), F32).at[0, :ne].set(b_expert[l]).at[0, ne:ne + ng].set(b_group[l])

        if l == 0:
            h = _ln_modulate(rows, xa, xb, m3)

        q, kcat, vcat, gm, gate_b, y = _mixer_in(rows, nblk, lay, h, w_all, g_q[l][None], w_uq_l, g_kv[l][None],
                                                 w_uk, w_uv, gm_ln_g[l][None], gm_ln_b[l][None],
                                                 w_s[l].astype(BF16), bs_exp, cs, nheads)
        cv = _conv_gate(rows, nblk, gate_b, y, w_conv[l])
        lk = ctx_len + seq
        attn = _attention(q, kcat, vcat, bsz, nheads, seq, min(tq, seq), 0, lk, 1, nrows)
        if not last:
            assert lk % ctx_len == 0
            tqc = min(tq, ctx_len)
            attn = _attention(q, kcat, vcat, bsz, nheads, ctx_len, tqc, n_lat // tqc, ctx_len, lk // ctx_len, nrows,
                              into=attn)
        merged = _merge(nrows, lay, h, attn, gm, cv, w_all, w_oa[l].astype(BF16), w_ob[l].astype(BF16),
                        w_oc[l].astype(BF16))
        x1, t_packed, meta, cnt = _post_mixer(rows, nblk, merged, xa, xb, w_o[l].astype(BF16), m3, ln1_g[l][None],
                                              ln1_b[l][None], w_r, b_r, alpha, ne, ng)

        e_ids = meta[:, 0:2].astype(jnp.int32)
        ranks = meta[:, 2:4].astype(jnp.int32)
        counts = cnt[0, :ne].astype(jnp.int32)
        experts = jnp.arange(ne, dtype=jnp.int32)
        padded = (counts + tme - 1) // tme * tme
        pend = jnp.cumsum(padded)
        pstart = pend - padded
        dest = (jnp.sum(jnp.where(e_ids[..., None] == experts, pstart, 0), axis=-1) + ranks).reshape(-1)
        nb = 2 * nrows // tme + ne
        n_used = pend[-1:] // tme
        blk = jnp.minimum(jnp.arange(nb, dtype=jnp.int32), n_used[0] - 1)
        block_e = jnp.minimum(jnp.sum(pend[None, :] <= (blk * tme)[:, None], axis=1), ne - 1).astype(jnp.int32)
        tail_blk = jnp.where(padded > 0, pend // tme - 1, -1).astype(jnp.int32)
        nonempty_from = lax.cummin(jnp.where(padded > 0, experts, ne), axis=0, reverse=True)
        next_nonempty = jnp.concatenate([nonempty_from[1:], jnp.full((1,), ne, jnp.int32)])
        next_e = next_nonempty[block_e]
        next_e = jnp.where(next_e >= ne, -1, next_e).astype(jnp.int32)

        buf = _dispatch(dest, tail_blk, t_packed, nb * tme, tm, tme)
        ybuf = _experts(block_e, n_used.astype(jnp.int32), next_e, buf, w1, w3, w2, l)
        m3_next = None if last else m_all[l + 1].reshape(mod_rows, 1, 6 * d)
        x2, h = _combine(rows, nblk, dest, ybuf, meta, x1, m3, ln2_g[l][None], ln2_b[l][None], m3_next, alpha)
        xa, xb = x2, x2
    return xa.reshape(bsz, seq, d)
```

```python
import functools

import jax
import jax.numpy as jnp
from jax import lax
from jax.experimental import pallas as pl
from jax.experimental.pallas import tpu as pltpu

F32 = jnp.float32
BF16 = jnp.bfloat16

GRID_W = 64
NOPE_DIM = 128
ROPE_DIM = 64
V_DIM = 128
ROPE_BASE = 10000.0
LN_EPS = 1e-6
HEAD_W = NOPE_DIM + 2 * ROPE_DIM
META_W = 128
LANES = 128
SUBLANES = 8
LOG2_E = 1.4426950408889634

CFG = dict(
    tm=256,
    tm_post=512,
    tm_post_sub=256,
    tq=2048,
    tq_sub=256,
    tm_merge=512,
    tn_merge=512,
    tn_mod=1024,
    tme=256,
    combine_chunk=32,
)
VMEM_LIMIT = 56 * 1024 * 1024


def _params(*sem):
    return pltpu.CompilerParams(dimension_semantics=sem, vmem_limit_bytes=VMEM_LIMIT)


def _dot(a, b):
    return jnp.dot(a, b, preferred_element_type=F32)


def _ln(x):
    mu = jnp.mean(x, axis=-1, keepdims=True)
    xc = x - mu
    var = jnp.mean(xc * xc, axis=-1, keepdims=True)
    return xc * lax.rsqrt(var + LN_EPS)


def _rms(x):
    return x * lax.rsqrt(jnp.mean(x * x, axis=-1, keepdims=True) + LN_EPS)


def _pack_rows(t):
    half = t.shape[1] // 2
    return pltpu.pack_elementwise([t[:, :half], t[:, half:]], packed_dtype=BF16)


def _unpack_rows(xp):
    lo = pltpu.unpack_elementwise(xp, index=0, packed_dtype=BF16, unpacked_dtype=F32)
    hi = pltpu.unpack_elementwise(xp, index=1, packed_dtype=BF16, unpacked_dtype=F32)
    return lo.astype(BF16), hi.astype(BF16)


def _packed_width(d):
    return d // 2


def _mod_kernel(c_ref, w_ref, b_ref, o_ref):
    c = c_ref[...]
    a = (c * jax.nn.sigmoid(c)).astype(BF16)
    o_ref[...] = _dot(a, w_ref[...].astype(BF16)) + b_ref[...]


def _mod_vectors(cc, w_mod, b_mod):
    nl, d, d6 = w_mod.shape
    rows = cc.shape[0]
    tn = CFG["tn_mod"]
    return pl.pallas_call(
        _mod_kernel,
        grid=(nl, d6 // tn),
        in_specs=[
            pl.BlockSpec((rows, d), lambda l, n: (0, 0)),
            pl.BlockSpec((None, d, tn), lambda l, n: (l, 0, n)),
            pl.BlockSpec((None, 1, tn), lambda l, n: (l, 0, n)),
        ],
        out_specs=pl.BlockSpec((None, rows, tn), lambda l, n: (l, 0, n)),
        out_shape=jax.ShapeDtypeStruct((nl, rows, d6), F32),
        compiler_params=_params("parallel", "parallel"),
        name="mod_vectors",
    )(cc, w_mod, b_mod.reshape(nl, 1, d6))


class _Rows:
    def __init__(self, bsz, seq, ctx_len, tm, ctx_blocks_may_span=False):
        assert seq % tm == 0 and (bsz * ctx_len) % tm == 0
        assert ctx_blocks_may_span or ctx_len % tm == 0
        self.bsz, self.seq, self.ctx_len, self.tm = bsz, seq, ctx_len, tm
        self.nsb = seq // tm
        self.ncb = ctx_len // tm
        self.nlat = bsz * self.nsb
        self.nctx = bsz * ctx_len // tm
        self.nall = self.nlat + self.nctx

    def mod_row(self, i):
        return jnp.where(i < self.nlat, i // self.nsb, self.bsz)

    def mod_spec(self, d, seg):
        return pl.BlockSpec((None, 1, d), lambda i: (self.mod_row(i), 0, seg))

    def rope_spec(self):
        return pl.BlockSpec((self.tm, 2 * ROPE_DIM), lambda i: (jnp.where(i < self.nlat, i % self.nsb, self.nsb), 0))

    def two_source_specs(self, d):
        return [
            pl.BlockSpec((self.tm, d), lambda i: (jnp.minimum(i, self.nlat - 1), 0)),
            pl.BlockSpec((self.tm, d), lambda i: (jnp.maximum(i - self.nlat, 0), 0), pipeline_mode=pl.Buffered(1)),
        ]

    def seq_edges(self, i):
        j = i - self.nlat
        first = jnp.where(i < self.nlat, i % self.nsb == 0, j % self.ncb == 0)
        last = jnp.where(i < self.nlat, i % self.nsb == self.nsb - 1, j % self.ncb == self.ncb - 1)
        return first, last


def _row_spec(tm, width):
    return pl.BlockSpec((tm, width), lambda i: (i, 0))


def _const_spec(shape):
    return pl.BlockSpec(shape, lambda i: (0,) * len(shape), pipeline_mode=pl.Buffered(1))


def _lnmod_kernel(xa_ref, xb_ref, sh_ref, sc_ref, h_ref, *, nlat):
    x = jnp.where(pl.program_id(0) < nlat, xa_ref[...], xb_ref[...])
    h_ref[...] = (_ln(x) * (1.0 + sc_ref[...]) + sh_ref[...]).astype(BF16)


def _ln_modulate(rows, xa, xb, m3):
    d = xa.shape[1]
    return pl.pallas_call(
        functools.partial(_lnmod_kernel, nlat=rows.nlat),
        grid=(rows.nall,),
        in_specs=rows.two_source_specs(d) + [rows.mod_spec(d, 0), rows.mod_spec(d, 1)],
        out_specs=_row_spec(rows.tm, d),
        out_shape=jax.ShapeDtypeStruct((rows.nall * rows.tm, d), BF16),
        compiler_params=_params("parallel"),
        name="ln_modulate",
    )(xa, xb, m3, m3)


class _InLayout:
    def __init__(self, d, ql, kvl, gwidth, cwidth, tn_gate):
        self.segs = {}
        off = 0
        for name, width, block in (("qa", ql, ql), ("ckv", kvl, kvl), ("gm", 2 * gwidth, 2 * gwidth),
                                   ("cb", cwidth, cwidth), ("cc", cwidth, cwidth), ("cx", cwidth, cwidth),
                                   ("gate", 3 * d, tn_gate), ("kr", 2 * ROPE_DIM, 2 * ROPE_DIM)):
            off = -(-off // block) * block
            self.segs[name] = (off, width, block)
            off += width
        self.width = -(-off // LANES) * LANES

    def block_index(self, name):
        off, _, block = self.segs[name]
        return off // block

    def spec(self, d, name):
        _, width, block = self.segs[name]
        assert width == block
        idx = self.block_index(name)
        return pl.BlockSpec((d, width), lambda i: (0, idx), pipeline_mode=pl.Buffered(1))


def _rope(x2, cs):
    w = x2 * cs
    return w + pltpu.roll(w, ROPE_DIM, axis=1)


def _mixer_in_kernel(h_ref, wqa_ref, wckv_ref, wkr_ref, wgm_ref, wcb_ref, wcc_ref, wcx_ref,
                     gq_ref, wuq_ref, gkv_ref, wuk_ref, wuv_ref, lg_ref, lb_ref, ws_ref, bs_ref, cs_ref,
                     q_ref, k_ref, v_ref, gm_ref, gb_ref, y_ref, *, nheads, nfull, width, chunk, groups):
    h = h_ref[...]
    cs = cs_ref[...]

    c = (_rms(_dot(h, wckv_ref[...])) * gkv_ref[...]).astype(BF16)
    kr = _rope(_dot(h, wkr_ref[...]), cs)
    lane = lax.broadcasted_iota(jnp.int32, kr.shape, 1)
    kr = jnp.where(lane < ROPE_DIM, kr, 0.0).astype(BF16)
    kn = _dot(c, wuk_ref[...]).astype(BF16)
    v_ref[...] = _dot(c, wuv_ref[...]).astype(BF16)
    for hh in range(nheads):
        k_ref[:, hh * HEAD_W:hh * HEAD_W + NOPE_DIM] = kn[:, hh * NOPE_DIM:(hh + 1) * NOPE_DIM]
        k_ref[:, hh * HEAD_W + NOPE_DIM:(hh + 1) * HEAD_W] = kr

    def rest():
        a = _rms(_dot(h, wqa_ref[...])) * gq_ref[...]
        q = _dot(a.astype(BF16), wuq_ref[...])
        for hh in range(nheads):
            base = hh * HEAD_W
            q_ref[:, base:base + NOPE_DIM] = q[:, base:base + NOPE_DIM].astype(BF16)
            q_ref[:, base + NOPE_DIM:base + HEAD_W] = _rope(q[:, base + NOPE_DIM:base + HEAD_W], cs).astype(BF16)
        z = jax.nn.gelu(_dot(h, wgm_ref[...]))
        u = z[:, :width]
        vv = (_ln(z[:, width:]) * lg_ref[...] + lb_ref[...]).astype(BF16)
        gw = width // groups
        for ci in range(z.shape[0] // chunk):
            r0 = ci * chunk
            for g in range(groups):
                c0 = g * gw
                mixed = _dot(ws_ref[g], vv[r0:r0 + chunk, c0:c0 + gw]) + bs_ref[g]
                gm_ref[r0:r0 + chunk, c0:c0 + gw] = (u[r0:r0 + chunk, c0:c0 + gw] * mixed).astype(BF16)
        gb_ref[...] = _dot(h, wcb_ref[...])
        y_ref[...] = _dot(h, wcc_ref[...]) * _dot(h, wcx_ref[...])

    if nfull is None:
        rest()
    else:
        pl.when(pl.program_id(0) < nfull)(rest)


def _mixer_in(rows, nblk, lay, h, w_all, g_q, w_uq, g_kv, w_uk, w_uv, ln_g, ln_b, w_s, bs_exp, cs, nheads):
    d = h.shape[1]
    tm = rows.tm
    ql = lay.segs["qa"][1]
    kvl = lay.segs["ckv"][1]
    width = lay.segs["gm"][1] // 2
    cw = lay.segs["cb"][1]
    groups, chunk, _ = w_s.shape
    assert tm % chunk == 0
    per_batch = rows.ncb + rows.nsb
    nrows = nblk * tm

    def kv_block(i):
        j = i - rows.nlat
        lat = (i // rows.nsb) * per_batch + rows.ncb + i % rows.nsb
        ctx = (j // rows.ncb) * per_batch + j % rows.ncb
        return jnp.where(i < rows.nlat, lat, ctx)

    def full_spec(w):
        return pl.BlockSpec((tm, w), lambda i: (jnp.minimum(i, nblk - 1), 0))

    def kv_spec(w):
        return pl.BlockSpec((tm, w), lambda i: (kv_block(i), 0))

    return pl.pallas_call(
        functools.partial(_mixer_in_kernel, nheads=nheads, nfull=None if nblk == rows.nall else nblk,
                          width=width, chunk=chunk, groups=groups),
        grid=(rows.nall,),
        in_specs=[_row_spec(tm, d)]
        + [lay.spec(d, name) for name in ("qa", "ckv", "kr", "gm", "cb", "cc", "cx")]
        + [_const_spec((1, ql)), _const_spec((ql, nheads * HEAD_W)),
           _const_spec((1, kvl)), _const_spec((kvl, nheads * NOPE_DIM)), _const_spec((kvl, nheads * V_DIM)),
           _const_spec((1, width)), _const_spec((1, width)),
           _const_spec((groups, chunk, chunk)), _const_spec((groups, chunk, width // groups)),
           rows.rope_spec()],
        out_specs=[full_spec(nheads * HEAD_W), kv_spec(nheads * HEAD_W), kv_spec(nheads * V_DIM),
                   full_spec(width), full_spec(cw), full_spec(cw)],
        out_shape=[
            jax.ShapeDtypeStruct((nrows, nheads * HEAD_W), BF16),
            jax.ShapeDtypeStruct((rows.nall * tm, nheads * HEAD_W), BF16),
            jax.ShapeDtypeStruct((rows.nall * tm, nheads * V_DIM), BF16),
            jax.ShapeDtypeStruct((nrows, width), BF16),
            jax.ShapeDtypeStruct((nrows, cw), F32),
            jax.ShapeDtypeStruct((nrows, cw), F32),
        ],
        compiler_params=_params("arbitrary"),
        name="mixer_in",
    )(h, *([w_all] * 7), g_q, w_uq, g_kv, w_uk, w_uv, ln_g, ln_b, w_s, bs_exp, cs)


def _conv_gate_kernel(gb_ref, y_ref, yp_ref, yn_ref, wc_ref, o_ref, *, rows):
    first, last = rows.seq_edges(pl.program_id(0))
    y = y_ref[...]
    y_prev = jnp.where(first, 0.0, yp_ref[SUBLANES - 1:SUBLANES, :])
    y_next = jnp.where(last, 0.0, yn_ref[0:1, :])
    tm = y.shape[0]
    row = lax.broadcasted_iota(jnp.int32, y.shape, 0)
    y_dn = jnp.where(row == 0, y_prev, pltpu.roll(y, 1, axis=0))
    y_up = jnp.where(row == tm - 1, y_next, pltpu.roll(y, tm - 1, axis=0))
    wc = wc_ref[...]
    conv = y_dn * wc[0:1, :] + y * wc[1:2, :] + y_up * wc[2:3, :]
    o_ref[...] = (gb_ref[...] * conv).astype(BF16)


def _conv_gate(rows, nblk, gate_b, y, w_conv):
    cw = y.shape[1]
    per = rows.tm // SUBLANES
    nhalo = y.shape[0] // SUBLANES
    return pl.pallas_call(
        functools.partial(_conv_gate_kernel, rows=rows),
        grid=(nblk,),
        in_specs=[
            _row_spec(rows.tm, cw),
            _row_spec(rows.tm, cw),
            pl.BlockSpec((SUBLANES, cw), lambda i: (jnp.maximum(i * per - 1, 0), 0)),
            pl.BlockSpec((SUBLANES, cw), lambda i: (jnp.minimum((i + 1) * per, nhalo - 1), 0)),
            _const_spec(w_conv.shape),
        ],
        out_specs=_row_spec(rows.tm, cw),
        out_shape=jax.ShapeDtypeStruct((nblk * rows.tm, cw), BF16),
        compiler_params=_params("parallel"),
        name="conv_gate",
    )(gate_b, y, y, y, w_conv)


def _attn_kernel(q_ref, k_ref, v_ref, *rest, scale, sub):
    o_ref = rest[-1]
    k = k_ref[...]
    v = v_ref[...]
    starts = list(range(0, q_ref.shape[0], sub))

    def scores(r0):
        return lax.dot_general(q_ref[r0:r0 + sub, :], k, (((1,), (1,)), ((), ())), preferred_element_type=F32)

    def weights(s):
        p = jnp.exp2((s - jnp.max(s, axis=-1, keepdims=True)) * (scale * LOG2_E))
        return p.astype(BF16), jnp.sum(p, axis=-1, keepdims=True)

    def emit(r0, p, denom):
        o_ref[r0:r0 + sub, :] = (_dot(p, v) / denom).astype(BF16)

    s_next = scores(starts[0])
    pending = None
    for c, r0 in enumerate(starts):
        s = s_next
        if c + 1 < len(starts):
            s_next = scores(starts[c + 1])
        current = (r0,) + weights(s)
        if pending is not None:
            emit(*pending)
        pending = current
    emit(*pending)


def _attention(q, k, v, bsz, nheads, lq, tq, q_block0, lk, k_block_stride, out_rows, into=None):
    nq = lq // tq
    in_specs = [
        pl.BlockSpec((tq, HEAD_W), lambda b, hh, qi: (q_block0 + b * nq + qi, hh)),
        pl.BlockSpec((lk, HEAD_W), lambda b, hh, qi: (b * k_block_stride, hh)),
        pl.BlockSpec((lk, V_DIM), lambda b, hh, qi: (b * k_block_stride, hh)),
    ]
    args = [q, k, v]
    aliases = {}
    if into is not None:
        in_specs.append(pl.BlockSpec(memory_space=pl.ANY))
        args.append(into)
        aliases = {3: 0}
    return pl.pallas_call(
        functools.partial(_attn_kernel, scale=float(NOPE_DIM + ROPE_DIM) ** -0.5, sub=min(tq, CFG["tq_sub"])),
        grid=(bsz, nheads, nq),
        in_specs=in_specs,
        out_specs=pl.BlockSpec((tq, V_DIM), lambda b, hh, qi: (q_block0 + b * nq + qi, hh)),
        out_shape=jax.ShapeDtypeStruct((out_rows, nheads * V_DIM), BF16),
        input_output_aliases=aliases,
        compiler_params=_params("parallel", "parallel", "parallel"),
        name="attention",
    )(*args)


def _merge_kernel(h_ref, a_ref, gm_ref, cv_ref, wg0_ref, wg1_ref, wg2_ref, woa_ref, wob_ref, woc_ref, o_ref):
    h = h_ref[...]
    m = jax.nn.sigmoid(_dot(h, wg0_ref[...])) * _dot(a_ref[...], woa_ref[...])
    m = m + jax.nn.sigmoid(_dot(h, wg1_ref[...])) * _dot(gm_ref[...], wob_ref[...])
    m = m + jax.nn.sigmoid(_dot(h, wg2_ref[...])) * _dot(cv_ref[...], woc_ref[...])
    o_ref[...] = m.astype(BF16)


def _merge(nrows, lay, h, attn, gm, cv, w_all, w_oa, w_ob, w_oc):
    d = h.shape[1]
    tm, tn = CFG["tm_merge"], CFG["tn_merge"]
    ncol = d // tn
    gate0 = lay.block_index("gate")

    def act(width):
        return pl.BlockSpec((tm, width), lambda n, i: (i, 0))

    def gate_w(k):
        return pl.BlockSpec((d, tn), lambda n, i: (0, gate0 + k * ncol + n))

    def out_w(width):
        return pl.BlockSpec((width, tn), lambda n, i: (0, n))

    return pl.pallas_call(
        _merge_kernel,
        grid=(ncol, nrows // tm),
        in_specs=[act(d), act(attn.shape[1]), act(gm.shape[1]), act(cv.shape[1]),
                  gate_w(0), gate_w(1), gate_w(2),
                  out_w(w_oa.shape[0]), out_w(w_ob.shape[0]), out_w(w_oc.shape[0])],
        out_specs=pl.BlockSpec((tm, tn), lambda n, i: (i, n)),
        out_shape=jax.ShapeDtypeStruct((nrows, d), BF16),
        compiler_params=_params("parallel", "parallel"),
        name="merge_branches",
    )(h, attn, gm, cv, w_all, w_all, w_all, w_oa, w_ob, w_oc)


def _route(logits, carry, ne, ng):
    tm = logits.shape[0]
    neg = -1e30
    lane = lax.broadcasted_iota(jnp.int32, logits.shape, 1).astype(F32)
    big = float(4 * META_W)

    def first_lane(mask):
        return jnp.min(jnp.where(mask, lane, big), axis=-1, keepdims=True)

    gl = jnp.where((lane >= ne) & (lane < ne + ng), logits, neg)
    gmax = jnp.max(gl, axis=-1, keepdims=True)
    g_w = 1.0 / jnp.sum(jnp.exp(gl - gmax), axis=-1, keepdims=True)
    g_idx = first_lane(gl == gmax) - ne
    epg = ne // ng
    in_group = (lane >= g_idx * epg) & (lane < (g_idx + 1) * epg)
    el = jnp.where(in_group, logits, neg)
    ee = jnp.where(in_group, jnp.exp(el - jnp.max(el, axis=-1, keepdims=True)), -1.0)
    v1 = jnp.max(ee, axis=-1, keepdims=True)
    i1 = first_lane(ee == v1)
    ee2 = jnp.where(lane == i1, -1.0, ee)
    v2 = jnp.max(ee2, axis=-1, keepdims=True)
    i2 = first_lane(ee2 == v2)
    w1 = g_w * v1 / (v1 + v2)
    w2 = g_w * v2 / (v1 + v2)

    hit1 = lane == i1
    hit2 = lane == i2
    onehot = jnp.where(hit1 | hit2, 1.0, 0.0)
    r_i = lax.broadcasted_iota(jnp.int32, (tm, tm), 0)
    c_i = lax.broadcasted_iota(jnp.int32, (tm, tm), 1)
    earlier = jnp.where(r_i > c_i, 1.0, 0.0).astype(BF16)
    base = carry + _dot(earlier, onehot.astype(BF16))
    rank1 = jnp.sum(jnp.where(hit1, base, 0.0), axis=-1, keepdims=True)
    rank2 = jnp.sum(jnp.where(hit2, base, 0.0), axis=-1, keepdims=True)
    new_carry = carry + jnp.sum(onehot, axis=0, keepdims=True)

    rec = jnp.zeros(logits.shape, F32)
    for k, val in enumerate((i1, i2, rank1, rank2, w1, w2)):
        rec = jnp.where(lane == float(k), val, rec)
    return rec, new_carry


def _post1_kernel(mg_ref, xa_ref, xb_ref, wo_ref, gate_ref, sh_ref, sc_ref, lg_ref, lb_ref, wr_ref, br_ref,
                  x1_ref, t_ref, meta_ref, cnt_ref, carry_ref, *, nlat, alpha, ne, ng, sub):
    i = pl.program_id(0)

    @pl.when(i == 0)
    def _():
        carry_ref[...] = jnp.zeros(carry_ref.shape, F32)

    carry = carry_ref[0:1, :]
    blocks = [slice(r0, r0 + sub) for r0 in range(0, mg_ref.shape[0], sub)]
    ys = [_dot(mg_ref[rs, :], wo_ref[...]) for rs in blocks]
    for rs, y in zip(blocks, ys):
        x = jnp.where(i < nlat, xa_ref[rs, :], xb_ref[rs, :])
        x1 = _ln(alpha * x + gate_ref[...] * y) * lg_ref[...] + lb_ref[...]
        x1_ref[rs, :] = x1
        t = _ln(x1) * (1.0 + sc_ref[...]) + sh_ref[...]
        t_ref[rs, :] = _pack_rows(t)
        logits = _dot(t.astype(BF16), wr_ref[...]) + br_ref[...]
        rec, carry = _route(logits, carry, ne, ng)
        meta_ref[rs, :] = rec
    carry_ref[...] = jnp.broadcast_to(carry, carry_ref.shape)
    cnt_ref[...] = jnp.broadcast_to(carry, cnt_ref.shape)


def _post_mixer(rows, nblk, merged, xa, xb, w_o, m3, ln_g, ln_b, w_r, b_r, alpha, ne, ng):
    d = merged.shape[1]
    tm = rows.tm
    nrows = nblk * tm
    pw = _packed_width(d)
    return pl.pallas_call(
        functools.partial(_post1_kernel, nlat=rows.nlat, alpha=alpha, ne=ne, ng=ng, sub=min(tm, CFG["tm_post_sub"])),
        grid=(nblk,),
        in_specs=[_row_spec(tm, d)] + rows.two_source_specs(d) + [
            _const_spec((d, d)),
            rows.mod_spec(d, 2), rows.mod_spec(d, 3), rows.mod_spec(d, 4),
            _const_spec((1, d)), _const_spec((1, d)),
            _const_spec((d, META_W)), _const_spec((1, META_W)),
        ],
        out_specs=[_row_spec(tm, d), _row_spec(tm, pw), _row_spec(tm, META_W),
                   pl.BlockSpec((SUBLANES, META_W), lambda i: (0, 0))],
        out_shape=[
            jax.ShapeDtypeStruct((nrows, d), F32),
            jax.ShapeDtypeStruct((nrows, pw), jnp.uint32),
            jax.ShapeDtypeStruct((nrows, META_W), F32),
            jax.ShapeDtypeStruct((SUBLANES, META_W), F32),
        ],
        scratch_shapes=[pltpu.VMEM((SUBLANES, META_W), F32)],
        compiler_params=_params("arbitrary"),
        name="post_mixer_router",
    )(merged, xa, xb, w_o, m3, m3, m3, ln_g, ln_b, w_r, b_r)


ISSUE_UNROLL = 8


def _dispatch_kernel(dest_ref, tail_ref, t_ref, buf_ref, zero_ref, sem, zsem, *, tm, tme, ne):
    i = pl.program_id(0)

    @pl.when(i == 0)
    def _():
        zero_ref[...] = jnp.zeros(zero_ref.shape, zero_ref.dtype)

        def zero_copy(e):
            row0 = pl.multiple_of(jnp.maximum(tail_ref[e], 0) * tme, tme)
            return pltpu.make_async_copy(zero_ref, buf_ref.at[pl.ds(row0, tme)], zsem)

        for e in range(ne):
            @pl.when(tail_ref[e] >= 0)
            def _():
                zero_copy(e).start()
        for e in range(ne):
            @pl.when(tail_ref[e] >= 0)
            def _():
                zero_copy(e).wait()

    base = i * (2 * tm)
    for r in range(tm):
        for k in range(2):
            pltpu.make_async_copy(t_ref.at[pl.ds(r, 1)], buf_ref.at[pl.ds(dest_ref[base + 2 * r + k], 1)], sem).start()
    for k in range(2):
        pltpu.make_async_copy(t_ref, buf_ref.at[pl.ds(0, tm)], sem).wait()


def _dispatch(dest, tail_blk, t_packed, buf_rows, tm, tme):
    nrows, pw = t_packed.shape
    return pl.pallas_call(
        functools.partial(_dispatch_kernel, tm=tm, tme=tme, ne=tail_blk.shape[0]),
        grid_spec=pltpu.PrefetchScalarGridSpec(
            num_scalar_prefetch=2,
            grid=(nrows // tm,),
            in_specs=[pl.BlockSpec((tm, pw), lambda i, dest, tail: (i, 0))],
            out_specs=pl.BlockSpec(memory_space=pl.ANY),
            scratch_shapes=[pltpu.VMEM((tme, pw), t_packed.dtype),
                            pltpu.SemaphoreType.DMA(()), pltpu.SemaphoreType.DMA(())],
        ),
        out_shape=jax.ShapeDtypeStruct((buf_rows, pw), t_packed.dtype),
        compiler_params=_params("arbitrary"),
        name="moe_dispatch",
    )(dest, tail_blk, t_packed)


def _expert_kernel(be_ref, nu_ref, nxt_ref, x_ref, w1_hbm, w3_hbm, w2_hbm, y_ref,
                   s1_ref, s3_ref, s2_ref, b1_ref, b3_ref, b2_ref, sem, *, layer):
    i = pl.program_id(0)

    def fetch(e):
        return (pltpu.make_async_copy(w1_hbm.at[layer, e], s1_ref, sem.at[0]),
                pltpu.make_async_copy(w3_hbm.at[layer, e], s3_ref, sem.at[1]),
                pltpu.make_async_copy(w2_hbm.at[layer, e], s2_ref, sem.at[2]))

    @pl.when(i < nu_ref[0])
    def _():
        e = be_ref[i]
        run_start = jnp.logical_or(i == 0, e != be_ref[jnp.maximum(i - 1, 0)])

        @pl.when(i == 0)
        def _():
            for cp in fetch(e):
                cp.start()

        @pl.when(run_start)
        def _():
            for cp in fetch(e):
                cp.wait()
            b1_ref[...] = s1_ref[...].astype(BF16)
            b3_ref[...] = s3_ref[...].astype(BF16)
            b2_ref[...] = s2_ref[...].astype(BF16)
            nxt = nxt_ref[i]

            @pl.when(nxt >= 0)
            def _():
                for cp in fetch(nxt):
                    cp.start()

        lo, hi = _unpack_rows(x_ref[...])
        half = lo.shape[1]

        def proj(b_ref):
            return _dot(lo, b_ref[:half, :]) + _dot(hi, b_ref[half:, :])

        a = proj(b1_ref)
        hid = (a * jax.nn.sigmoid(a) * proj(b3_ref)).astype(BF16)
        y_ref[...] = _dot(hid, b2_ref[...])


def _experts(block_e, n_used, next_e, buf, w1, w3, w2, layer):
    tme = CFG["tme"]
    nb = buf.shape[0] // tme
    pw = buf.shape[1]
    _, _, d, eh = w1.shape

    def row_block(i, be, nu, nxt):
        return (jnp.minimum(i, nu[0] - 1), 0)

    hbm = pl.BlockSpec(memory_space=pl.ANY)
    return pl.pallas_call(
        functools.partial(_expert_kernel, layer=layer),
        grid_spec=pltpu.PrefetchScalarGridSpec(
            num_scalar_prefetch=3,
            grid=(nb,),
            in_specs=[pl.BlockSpec((tme, pw), row_block), hbm, hbm, hbm],
            out_specs=pl.BlockSpec((tme, d), row_block),
            scratch_shapes=[pltpu.VMEM((d, eh), F32), pltpu.VMEM((d, eh), F32), pltpu.VMEM((eh, d), F32),
                            pltpu.VMEM((d, eh), BF16), pltpu.VMEM((d, eh), BF16), pltpu.VMEM((eh, d), BF16),
                            pltpu.SemaphoreType.DMA((3,))],
        ),
        out_shape=jax.ShapeDtypeStruct((nb * tme, d), F32),
        compiler_params=_params("arbitrary"),
        name="experts",
    )(block_e, n_used, next_e, buf, w1, w3, w2)


def _combine_kernel(dest_ref, y_ref, meta_ref, x1_ref, gate_ref, lg_ref, lb_ref, sh_ref, sc_ref, *rest,
                    tm, alpha, emit_h, chunk):
    if emit_h:
        x2_ref, h_ref, gbuf, sem = rest
    else:
        x2_ref, gbuf, sem = rest
    i = pl.program_id(0)
    nsteps = pl.num_programs(0)

    def start_row(blk, slot, r):
        a = (blk * tm + r) * 2
        for k in range(2):
            pltpu.make_async_copy(y_ref.at[pl.ds(dest_ref[a + k], 1)], gbuf.at[slot, k, pl.ds(r, 1)],
                                  sem.at[slot]).start()

    def wait_slot(slot):
        for k in range(2):
            pltpu.make_async_copy(y_ref.at[pl.ds(0, tm)], gbuf.at[slot, k], sem.at[slot]).wait()

    @pl.when(i == 0)
    def _():
        def body(r, carry):
            start_row(0, 0, r)
            return carry
        lax.fori_loop(0, tm, body, 0, unroll=ISSUE_UNROLL)

    slot = i % 2
    wait_slot(slot)

    nxt = jnp.minimum(i + 1, nsteps - 1)
    for r0 in range(0, tm, chunk):
        for r in range(r0, r0 + chunk):
            start_row(nxt, 1 - slot, r)
        rs = slice(r0, r0 + chunk)
        meta = meta_ref[rs, :]
        f = meta[:, 4:5] * gbuf[slot, 0, rs, :] + meta[:, 5:6] * gbuf[slot, 1, rs, :]
        x2 = _ln(alpha * x1_ref[rs, :] + gate_ref[...] * f) * lg_ref[...] + lb_ref[...]
        x2_ref[rs, :] = x2
        if emit_h:
            h_ref[rs, :] = (_ln(x2) * (1.0 + sc_ref[...]) + sh_ref[...]).astype(BF16)

    @pl.when(i == nsteps - 1)
    def _():
        wait_slot(1 - slot)


def _combine(rows, nblk, dest, ybuf, meta, x1, m3, ln_g, ln_b, m3_next, alpha):
    d = x1.shape[1]
    tm = rows.tm
    nrows = nblk * tm
    emit_h = m3_next is not None
    if not emit_h:
        m3_next = m3

    def rspec(width):
        return pl.BlockSpec((tm, width), lambda i, dest: (i, 0))

    def cspec(shape):
        return pl.BlockSpec(shape, lambda i, dest: (0,) * len(shape))

    def mspec(seg):
        return pl.BlockSpec((None, 1, d), lambda i, dest: (rows.mod_row(i), 0, seg))

    out_specs = [rspec(d)]
    out_shape = [jax.ShapeDtypeStruct((nrows, d), F32)]
    if emit_h:
        out_specs.append(rspec(d))
        out_shape.append(jax.ShapeDtypeStruct((nrows, d), BF16))
    out = pl.pallas_call(
        functools.partial(_combine_kernel, tm=tm, alpha=alpha, emit_h=emit_h, chunk=min(tm, CFG["combine_chunk"])),
        grid_spec=pltpu.PrefetchScalarGridSpec(
            num_scalar_prefetch=1,
            grid=(nblk,),
            in_specs=[pl.BlockSpec(memory_space=pl.ANY), rspec(META_W), rspec(d), mspec(5),
                      cspec((1, d)), cspec((1, d)), mspec(0), mspec(1)],
            out_specs=out_specs,
            scratch_shapes=[pltpu.VMEM((2, 2, tm, d), F32), pltpu.SemaphoreType.DMA((2,))],
        ),
        out_shape=out_shape,
        compiler_params=_params("arbitrary"),
        name="moe_combine",
    )(dest, ybuf, meta, x1, m3, ln_g, ln_b, m3_next, m3_next)
    return out if emit_h else (out[0], None)


def _rot_cols(w):
    q = ROPE_DIM // 4
    return jnp.concatenate([-w[..., q:2 * q], w[..., 0:q], -w[..., 3 * q:4 * q], w[..., 2 * q:3 * q]], axis=-1)


KIND_COPY, KIND_ROPE, KIND_ZERO = 0, 1, 2


def _reorder_kernel(ia_ref, ib_ref, kind_ref, a_ref, b_ref, o_ref):
    del ia_ref, ib_ref
    kind = kind_ref[pl.program_id(0)]
    a = a_ref[...]

    @pl.when(kind == KIND_COPY)
    def _():
        o_ref[...] = jnp.concatenate([a, b_ref[...]], axis=0).T.astype(BF16)

    @pl.when(kind == KIND_ROPE)
    def _():
        q = ROPE_DIM // 4
        rot = jnp.concatenate([-a[q:2 * q], a[0:q], -a[3 * q:4 * q], a[2 * q:3 * q]], axis=0)
        o_ref[...] = jnp.concatenate([a, rot], axis=0).T.astype(BF16)

    @pl.when(kind == KIND_ZERO)
    def _():
        o_ref[...] = jnp.zeros(o_ref.shape, BF16)


def _reorder_w_in(w_in, layer, lay, ql, kvl, gwidth, cwidth):
    _, d, p_in = w_in.shape
    piece = ROPE_DIM
    off_kr = ql + kvl
    off_gm = off_kr + ROPE_DIM
    off_cv = off_gm + 2 * gwidth
    off_gate = off_cv + 3 * cwidth
    src_of = {"qa": 0, "ckv": ql, "gm": off_gm, "cb": off_cv, "cc": off_cv + cwidth, "cx": off_cv + 2 * cwidth,
              "gate": off_gate, "kr": off_kr}
    nblk = lay.width // LANES
    ia = [0] * nblk
    ib = [0] * nblk
    kind = [KIND_ZERO] * nblk
    for name, (off, width, _) in lay.segs.items():
        assert off % LANES == 0 and src_of[name] % piece == 0
        for j in range(width // LANES):
            src = src_of[name] + j * LANES
            ia[off // LANES + j] = src // piece
            ib[off // LANES + j] = src // piece if name == "kr" else src // piece + 1
            kind[off // LANES + j] = KIND_ROPE if name == "kr" else KIND_COPY
    tables = [jnp.asarray(t, dtype=jnp.int32) for t in (ia, ib, kind)]
    return pl.pallas_call(
        _reorder_kernel,
        grid_spec=pltpu.PrefetchScalarGridSpec(
            num_scalar_prefetch=3,
            grid=(nblk,),
            in_specs=[pl.BlockSpec((None, piece, d), lambda j, ia, ib, kind: (layer, ia[j], 0)),
                      pl.BlockSpec((None, piece, d), lambda j, ia, ib, kind: (layer, ib[j], 0))],
            out_specs=pl.BlockSpec((d, LANES), lambda j, ia, ib, kind: (0, j)),
        ),
        out_shape=jax.ShapeDtypeStruct((d, lay.width), BF16),
        compiler_params=_params("arbitrary"),
        name="reorder_w_in",
    )(*tables, *([jnp.swapaxes(w_in, 1, 2)] * 2))


def _rope_table(seq, tm):
    n_rows = seq // GRID_W
    t = jnp.arange(n_rows * GRID_W)
    row = (t // GRID_W).astype(F32)
    col = (t % GRID_W).astype(F32)
    half = ROPE_DIM // 2
    inv = ROPE_BASE ** (-jnp.arange(0, half, 2, dtype=F32) / half)
    ang = jnp.concatenate([row[:, None] * inv] * 2 + [col[:, None] * inv] * 2, axis=-1)
    cs = jnp.concatenate([jnp.cos(ang), jnp.sin(ang)], axis=-1)
    ident = jnp.concatenate([jnp.ones((tm, ROPE_DIM), F32), jnp.zeros((tm, ROPE_DIM), F32)], axis=-1)
    return jnp.concatenate([cs, ident], axis=0)


def kernel(x, c, ctx, c_ctx, w_mod, b_mod, w_in, g_q, w_uq, g_kv, w_ukv, w_oa, gm_ln_g, gm_ln_b, w_s, b_s, w_ob,
           w_conv, w_oc, w_o, ln1_g, ln1_b, w_group, b_group, w_expert, b_expert, w1, w3, w2, ln2_g, ln2_b):
    bsz, seq, d = x.shape
    ctx_len = ctx.shape[1]
    depth = w_mod.shape[0]
    ql = g_q.shape[1]
    kvl = g_kv.shape[1]
    nheads = w_uq.shape[2] // (NOPE_DIM + ROPE_DIM)
    gwidth = gm_ln_g.shape[1]
    cwidth = w_conv.shape[2]
    ne = w_expert.shape[2]
    ng = w_group.shape[2]
    alpha = float((2 * depth) ** 0.25)
    tm, tq, tme = CFG["tm"], CFG["tq"], CFG["tme"]
    rows = _Rows(bsz, seq, ctx_len, tm)
    rows_post = _Rows(bsz, seq, ctx_len, CFG["tm_post"], ctx_blocks_may_span=True)
    lay = _InLayout(d, ql, kvl, gwidth, cwidth, CFG["tn_merge"])
    n_lat, n_ctx = bsz * seq, bsz * ctx_len

    mod_rows = -(-(bsz + 1) // SUBLANES) * SUBLANES
    cc = jnp.zeros((mod_rows, d), F32).at[:bsz].set(c).at[bsz].set(c_ctx)
    m_all = _mod_vectors(cc, w_mod, b_mod)
    cs = _rope_table(seq, tm)

    xa = x.reshape(n_lat, d)
    xb = ctx.reshape(n_ctx, d)
    h = None
    for l in range(depth):
        last = l == depth - 1
        m3 = m_all[l].reshape(mod_rows, 1, 6 * d)
        nblk = rows.nlat if last else rows.nall
        nrows = nblk * tm

        w_all = _reorder_w_in(w_in, l, lay, ql, kvl, gwidth, cwidth)
        uq = w_uq[l].reshape(ql, nheads, NOPE_DIM + ROPE_DIM)
        uq_rope = uq[..., NOPE_DIM:]
        w_uq_l = jnp.concatenate([uq, _rot_cols(uq_rope)], axis=-1).reshape(ql, nheads * HEAD_W).astype(BF16)
        ukv = w_ukv[l].reshape(kvl, nheads, NOPE_DIM + V_DIM)
        w_uk = ukv[..., :NOPE_DIM].reshape(kvl, nheads * NOPE_DIM).astype(BF16)
        w_uv = ukv[..., NOPE_DIM:].reshape(kvl, nheads * V_DIM).astype(BF16)
        bs_exp = jnp.broadcast_to(b_s[l][:, :, None], b_s.shape[1:] + (gwidth // w_s.shape[1],))
        w_r = jnp.zeros((d, META_W), F32).at[:, :ne].set(w_expert[l]).at[:, ne:ne + ng].set(w_group[l]).astype(BF16)
        b_r = jnp.zeros((1, META_W), F32).at[0, :ne].set(b_expert[l]).at[0, ne:ne + ng].set(b_group[l])

        if l == 0:
            h = _ln_modulate(rows, xa, xb, m3)

        q, kcat, vcat, gm, gate_b, y = _mixer_in(rows, nblk, lay, h, w_all, g_q[l][None], w_uq_l, g_kv[l][None],
                                                 w_uk, w_uv, gm_ln_g[l][None], gm_ln_b[l][None],
                                                 w_s[l].astype(BF16), bs_exp, cs, nheads)
        cv = _conv_gate(rows, nblk, gate_b, y, w_conv[l])
        lk = ctx_len + seq
        attn = _attention(q, kcat, vcat, bsz, nheads, seq, min(tq, seq), 0, lk, 1, nrows)
        if not last:
            assert lk % ctx_len == 0
            tqc = min(tq, ctx_len)
            attn = _attention(q, kcat, vcat, bsz, nheads, ctx_len, tqc, n_lat // tqc, ctx_len, lk // ctx_len, nrows,
                              into=attn)
        merged = _merge(nrows, lay, h, attn, gm, cv, w_all, w_oa[l].astype(BF16), w_ob[l].astype(BF16),
                        w_oc[l].astype(BF16))
        x1, t_packed, meta, cnt = _post_mixer(rows_post, nrows // rows_post.tm, merged, xa, xb, w_o[l].astype(BF16),
                                              m3, ln1_g[l][None], ln1_b[l][None], w_r, b_r, alpha, ne, ng)

        e_ids = meta[:, 0:2].astype(jnp.int32)
        ranks = meta[:, 2:4].astype(jnp.int32)
        counts = cnt[0, :ne].astype(jnp.int32)
        experts = jnp.arange(ne, dtype=jnp.int32)
        padded = (counts + tme - 1) // tme * tme
        pend = jnp.cumsum(padded)
        pstart = pend - padded
        dest = (jnp.sum(jnp.where(e_ids[..., None] == experts, pstart, 0), axis=-1) + ranks).reshape(-1)
        nb = 2 * nrows // tme + ne
        n_used = pend[-1:] // tme
        blk = jnp.minimum(jnp.arange(nb, dtype=jnp.int32), n_used[0] - 1)
        block_e = jnp.minimum(jnp.sum(pend[None, :] <= (blk * tme)[:, None], axis=1), ne - 1).astype(jnp.int32)
        tail_blk = jnp.where(padded > 0, pend // tme - 1, -1).astype(jnp.int32)
        nonempty_from = lax.cummin(jnp.where(padded > 0, experts, ne), axis=0, reverse=True)
        next_nonempty = jnp.concatenate([nonempty_from[1:], jnp.full((1,), ne, jnp.int32)])
        next_e = next_nonempty[block_e]
        next_e = jnp.where(next_e >= ne, -1, next_e).astype(jnp.int32)

        buf = _dispatch(dest, tail_blk, t_packed, nb * tme, tm, tme)
        ybuf = _experts(block_e, n_used.astype(jnp.int32), next_e, buf, w1, w3, w2, l)
        m3_next = None if last else m_all[l + 1].reshape(mod_rows, 1, 6 * d)
        x2, h = _combine(rows, nblk, dest, ybuf, meta, x1, m3, ln2_g[l][None], ln2_b[l][None], m3_next, alpha)
        xa, xb = x2, x2
    return xa.reshape(bsz, seq, d)
```

```python
import functools

import jax
import jax.numpy as jnp
from jax import lax
from jax.experimental import pallas as pl
from jax.experimental.pallas import tpu as pltpu

F32 = jnp.float32
BF16 = jnp.bfloat16

GRID_W = 64
NOPE_DIM = 128
ROPE_DIM = 64
V_DIM = 128
ROPE_BASE = 10000.0
LN_EPS = 1e-6
HEAD_W = NOPE_DIM + 2 * ROPE_DIM
META_W = 128
LANES = 128
SUBLANES = 8
LOG2_E = 1.4426950408889634

CFG = dict(
    tm=256,
    tm_post=512,
    tm_post_sub=256,
    tq=2048,
    tq_sub=256,
    tm_merge=512,
    tn_merge=512,
    tn_mod=1024,
    tme=256,
    combine_chunk=32,
)
VMEM_LIMIT = 56 * 1024 * 1024


def _params(*sem):
    return pltpu.CompilerParams(dimension_semantics=sem, vmem_limit_bytes=VMEM_LIMIT)


def _dot(a, b):
    return jnp.dot(a, b, preferred_element_type=F32)


def _ln(x):
    mu = jnp.mean(x, axis=-1, keepdims=True)
    xc = x - mu
    var = jnp.mean(xc * xc, axis=-1, keepdims=True)
    return xc * lax.rsqrt(var + LN_EPS)


def _rms(x):
    return x * lax.rsqrt(jnp.mean(x * x, axis=-1, keepdims=True) + LN_EPS)


def _pack_rows(t):
    half = t.shape[1] // 2
    return pltpu.pack_elementwise([t[:, :half], t[:, half:]], packed_dtype=BF16)


def _unpack_rows(xp):
    lo = pltpu.unpack_elementwise(xp, index=0, packed_dtype=BF16, unpacked_dtype=F32)
    hi = pltpu.unpack_elementwise(xp, index=1, packed_dtype=BF16, unpacked_dtype=F32)
    return lo.astype(BF16), hi.astype(BF16)


def _packed_width(d):
    return d // 2


def _mod_kernel(c_ref, w_ref, b_ref, o_ref):
    c = c_ref[...]
    a = (c * jax.nn.sigmoid(c)).astype(BF16)
    o_ref[...] = _dot(a, w_ref[...].astype(BF16)) + b_ref[...]


def _mod_vectors(cc, w_mod, b_mod):
    nl, d, d6 = w_mod.shape
    rows = cc.shape[0]
    tn = CFG["tn_mod"]
    return pl.pallas_call(
        _mod_kernel,
        grid=(nl, d6 // tn),
        in_specs=[
            pl.BlockSpec((rows, d), lambda l, n: (0, 0)),
            pl.BlockSpec((None, d, tn), lambda l, n: (l, 0, n)),
            pl.BlockSpec((None, 1, tn), lambda l, n: (l, 0, n)),
        ],
        out_specs=pl.BlockSpec((None, rows, tn), lambda l, n: (l, 0, n)),
        out_shape=jax.ShapeDtypeStruct((nl, rows, d6), F32),
        compiler_params=_params("parallel", "parallel"),
        name="mod_vectors",
    )(cc, w_mod, b_mod.reshape(nl, 1, d6))


class _Rows:
    def __init__(self, bsz, seq, ctx_len, tm, ctx_blocks_may_span=False):
        assert seq % tm == 0 and (bsz * ctx_len) % tm == 0
        assert ctx_blocks_may_span or ctx_len % tm == 0
        self.bsz, self.seq, self.ctx_len, self.tm = bsz, seq, ctx_len, tm
        self.nsb = seq // tm
        self.ncb = ctx_len // tm
        self.nlat = bsz * self.nsb
        self.nctx = bsz * ctx_len // tm
        self.nall = self.nlat + self.nctx

    def mod_row(self, i):
        return jnp.where(i < self.nlat, i // self.nsb, self.bsz)

    def mod_spec(self, d, seg):
        return pl.BlockSpec((None, 1, d), lambda i: (self.mod_row(i), 0, seg))

    def rope_spec(self):
        return pl.BlockSpec((self.tm, 2 * ROPE_DIM), lambda i: (jnp.where(i < self.nlat, i % self.nsb, self.nsb), 0))

    def two_source_specs(self, d):
        return [
            pl.BlockSpec((self.tm, d), lambda i: (jnp.minimum(i, self.nlat - 1), 0)),
            pl.BlockSpec((self.tm, d), lambda i: (jnp.maximum(i - self.nlat, 0), 0), pipeline_mode=pl.Buffered(1)),
        ]

    def seq_edges(self, i):
        j = i - self.nlat
        first = jnp.where(i < self.nlat, i % self.nsb == 0, j % self.ncb == 0)
        last = jnp.where(i < self.nlat, i % self.nsb == self.nsb - 1, j % self.ncb == self.ncb - 1)
        return first, last


def _row_spec(tm, width):
    return pl.BlockSpec((tm, width), lambda i: (i, 0))


def _const_spec(shape):
    return pl.BlockSpec(shape, lambda i: (0,) * len(shape), pipeline_mode=pl.Buffered(1))


def _lnmod_kernel(xa_ref, xb_ref, sh_ref, sc_ref, h_ref, *, nlat):
    x = jnp.where(pl.program_id(0) < nlat, xa_ref[...], xb_ref[...])
    h_ref[...] = (_ln(x) * (1.0 + sc_ref[...]) + sh_ref[...]).astype(BF16)


def _ln_modulate(rows, xa, xb, m3):
    d = xa.shape[1]
    return pl.pallas_call(
        functools.partial(_lnmod_kernel, nlat=rows.nlat),
        grid=(rows.nall,),
        in_specs=rows.two_source_specs(d) + [rows.mod_spec(d, 0), rows.mod_spec(d, 1)],
        out_specs=_row_spec(rows.tm, d),
        out_shape=jax.ShapeDtypeStruct((rows.nall * rows.tm, d), BF16),
        compiler_params=_params("parallel"),
        name="ln_modulate",
    )(xa, xb, m3, m3)


class _InLayout:
    def __init__(self, d, ql, kvl, gwidth, cwidth, tn_gate):
        self.segs = {}
        off = 0
        for name, width, block in (("qa", ql, ql), ("ckv", kvl, kvl), ("gm", 2 * gwidth, 2 * gwidth),
                                   ("cb", cwidth, cwidth), ("cc", cwidth, cwidth), ("cx", cwidth, cwidth),
                                   ("gate", 3 * d, tn_gate), ("kr", 2 * ROPE_DIM, 2 * ROPE_DIM)):
            off = -(-off // block) * block
            self.segs[name] = (off, width, block)
            off += width
        self.width = -(-off // LANES) * LANES

    def block_index(self, name):
        off, _, block = self.segs[name]
        return off // block

    def spec(self, d, name):
        _, width, block = self.segs[name]
        assert width == block
        idx = self.block_index(name)
        return pl.BlockSpec((d, width), lambda i: (0, idx), pipeline_mode=pl.Buffered(1))


def _rope(x2, cs):
    w = x2 * cs
    return w + pltpu.roll(w, ROPE_DIM, axis=1)


def _mixer_in_kernel(h_ref, wqa_ref, wckv_ref, wkr_ref, wgm_ref, wcb_ref, wcc_ref, wcx_ref,
                     gq_ref, wuq_ref, gkv_ref, wuk_ref, wuv_ref, lg_ref, lb_ref, ws_ref, bs_ref, cs_ref,
                     q_ref, k_ref, v_ref, gm_ref, gb_ref, y_ref, *, nheads, nfull, width, chunk, groups):
    h = h_ref[...]
    cs = cs_ref[...]

    c = (_rms(_dot(h, wckv_ref[...])) * gkv_ref[...]).astype(BF16)
    kr = _rope(_dot(h, wkr_ref[...]), cs)
    lane = lax.broadcasted_iota(jnp.int32, kr.shape, 1)
    kr = jnp.where(lane < ROPE_DIM, kr, 0.0).astype(BF16)
    kn = _dot(c, wuk_ref[...]).astype(BF16)
    v_ref[...] = _dot(c, wuv_ref[...]).astype(BF16)
    for hh in range(nheads):
        k_ref[:, hh * HEAD_W:hh * HEAD_W + NOPE_DIM] = kn[:, hh * NOPE_DIM:(hh + 1) * NOPE_DIM]
        k_ref[:, hh * HEAD_W + NOPE_DIM:(hh + 1) * HEAD_W] = kr

    def rest():
        a = _rms(_dot(h, wqa_ref[...])) * gq_ref[...]
        q = _dot(a.astype(BF16), wuq_ref[...])
        for hh in range(nheads):
            base = hh * HEAD_W
            q_ref[:, base:base + NOPE_DIM] = q[:, base:base + NOPE_DIM].astype(BF16)
            q_ref[:, base + NOPE_DIM:base + HEAD_W] = _rope(q[:, base + NOPE_DIM:base + HEAD_W], cs).astype(BF16)
        z = jax.nn.gelu(_dot(h, wgm_ref[...]))
        u = z[:, :width]
        vv = (_ln(z[:, width:]) * lg_ref[...] + lb_ref[...]).astype(BF16)
        gw = width // groups
        for ci in range(z.shape[0] // chunk):
            r0 = ci * chunk
            for g in range(groups):
                c0 = g * gw
                mixed = _dot(ws_ref[g], vv[r0:r0 + chunk, c0:c0 + gw]) + bs_ref[g]
                gm_ref[r0:r0 + chunk, c0:c0 + gw] = (u[r0:r0 + chunk, c0:c0 + gw] * mixed).astype(BF16)
        gb_ref[...] = _dot(h, wcb_ref[...])
        y_ref[...] = _dot(h, wcc_ref[...]) * _dot(h, wcx_ref[...])

    if nfull is None:
        rest()
    else:
        pl.when(pl.program_id(0) < nfull)(rest)


def _mixer_in(rows, nblk, lay, h, w_all, g_q, w_uq, g_kv, w_uk, w_uv, ln_g, ln_b, w_s, bs_exp, cs, nheads):
    d = h.shape[1]
    tm = rows.tm
    ql = lay.segs["qa"][1]
    kvl = lay.segs["ckv"][1]
    width = lay.segs["gm"][1] // 2
    cw = lay.segs["cb"][1]
    groups, chunk, _ = w_s.shape
    assert tm % chunk == 0
    per_batch = rows.ncb + rows.nsb
    nrows = nblk * tm

    def kv_block(i):
        j = i - rows.nlat
        lat = (i // rows.nsb) * per_batch + rows.ncb + i % rows.nsb
        ctx = (j // rows.ncb) * per_batch + j % rows.ncb
        return jnp.where(i < rows.nlat, lat, ctx)

    def full_spec(w):
        return pl.BlockSpec((tm, w), lambda i: (jnp.minimum(i, nblk - 1), 0))

    def kv_spec(w):
        return pl.BlockSpec((tm, w), lambda i: (kv_block(i), 0))

    return pl.pallas_call(
        functools.partial(_mixer_in_kernel, nheads=nheads, nfull=None if nblk == rows.nall else nblk,
                          width=width, chunk=chunk, groups=groups),
        grid=(rows.nall,),
        in_specs=[_row_spec(tm, d)]
        + [lay.spec(d, name) for name in ("qa", "ckv", "kr", "gm", "cb", "cc", "cx")]
        + [_const_spec((1, ql)), _const_spec((ql, nheads * HEAD_W)),
           _const_spec((1, kvl)), _const_spec((kvl, nheads * NOPE_DIM)), _const_spec((kvl, nheads * V_DIM)),
           _const_spec((1, width)), _const_spec((1, width)),
           _const_spec((groups, chunk, chunk)), _const_spec((groups, chunk, width // groups)),
           rows.rope_spec()],
        out_specs=[full_spec(nheads * HEAD_W), kv_spec(nheads * HEAD_W), kv_spec(nheads * V_DIM),
                   full_spec(width), full_spec(cw), full_spec(cw)],
        out_shape=[
            jax.ShapeDtypeStruct((nrows, nheads * HEAD_W), BF16),
            jax.ShapeDtypeStruct((rows.nall * tm, nheads * HEAD_W), BF16),
            jax.ShapeDtypeStruct((rows.nall * tm, nheads * V_DIM), BF16),
            jax.ShapeDtypeStruct((nrows, width), BF16),
            jax.ShapeDtypeStruct((nrows, cw), F32),
            jax.ShapeDtypeStruct((nrows, cw), F32),
        ],
        compiler_params=_params("arbitrary"),
        name="mixer_in",
    )(h, *([w_all] * 7), g_q, w_uq, g_kv, w_uk, w_uv, ln_g, ln_b, w_s, bs_exp, cs)


def _conv_gate_kernel(gb_ref, y_ref, yp_ref, yn_ref, wc_ref, o_ref, *, rows):
    first, last = rows.seq_edges(pl.program_id(0))
    y = y_ref[...]
    y_prev = jnp.where(first, 0.0, yp_ref[SUBLANES - 1:SUBLANES, :])
    y_next = jnp.where(last, 0.0, yn_ref[0:1, :])
    tm = y.shape[0]
    row = lax.broadcasted_iota(jnp.int32, y.shape, 0)
    y_dn = jnp.where(row == 0, y_prev, pltpu.roll(y, 1, axis=0))
    y_up = jnp.where(row == tm - 1, y_next, pltpu.roll(y, tm - 1, axis=0))
    wc = wc_ref[...]
    conv = y_dn * wc[0:1, :] + y * wc[1:2, :] + y_up * wc[2:3, :]
    o_ref[...] = (gb_ref[...] * conv).astype(BF16)


def _conv_gate(rows, nblk, gate_b, y, w_conv):
    cw = y.shape[1]
    per = rows.tm // SUBLANES
    nhalo = y.shape[0] // SUBLANES
    return pl.pallas_call(
        functools.partial(_conv_gate_kernel, rows=rows),
        grid=(nblk,),
        in_specs=[
            _row_spec(rows.tm, cw),
            _row_spec(rows.tm, cw),
            pl.BlockSpec((SUBLANES, cw), lambda i: (jnp.maximum(i * per - 1, 0), 0)),
            pl.BlockSpec((SUBLANES, cw), lambda i: (jnp.minimum((i + 1) * per, nhalo - 1), 0)),
            _const_spec(w_conv.shape),
        ],
        out_specs=_row_spec(rows.tm, cw),
        out_shape=jax.ShapeDtypeStruct((nblk * rows.tm, cw), BF16),
        compiler_params=_params("parallel"),
        name="conv_gate",
    )(gate_b, y, y, y, w_conv)


def _attn_kernel(q_ref, k_ref, v_ref, *rest, scale, sub):
    o_ref = rest[-1]
    k = k_ref[...]
    v = v_ref[...]
    starts = list(range(0, q_ref.shape[0], sub))

    def scores(r0):
        return lax.dot_general(q_ref[r0:r0 + sub, :], k, (((1,), (1,)), ((), ())), preferred_element_type=F32)

    def weights(s):
        p = jnp.exp2((s - jnp.max(s, axis=-1, keepdims=True)) * (scale * LOG2_E))
        return p.astype(BF16), jnp.sum(p, axis=-1, keepdims=True)

    def emit(r0, p, denom):
        o_ref[r0:r0 + sub, :] = (_dot(p, v) / denom).astype(BF16)

    s_next = scores(starts[0])
    pending = None
    for c, r0 in enumerate(starts):
        s = s_next
        if c + 1 < len(starts):
            s_next = scores(starts[c + 1])
        current = (r0,) + weights(s)
        if pending is not None:
            emit(*pending)
        pending = current
    emit(*pending)


def _attention(q, k, v, bsz, nheads, lq, tq, q_block0, lk, k_block_stride, out_rows, into=None):
    nq = lq // tq
    in_specs = [
        pl.BlockSpec((tq, HEAD_W), lambda b, hh, qi: (q_block0 + b * nq + qi, hh)),
        pl.BlockSpec((lk, HEAD_W), lambda b, hh, qi: (b * k_block_stride, hh)),
        pl.BlockSpec((lk, V_DIM), lambda b, hh, qi: (b * k_block_stride, hh)),
    ]
    args = [q, k, v]
    aliases = {}
    if into is not None:
        in_specs.append(pl.BlockSpec(memory_space=pl.ANY))
        args.append(into)
        aliases = {3: 0}
    return pl.pallas_call(
        functools.partial(_attn_kernel, scale=float(NOPE_DIM + ROPE_DIM) ** -0.5, sub=min(tq, CFG["tq_sub"])),
        grid=(bsz, nheads, nq),
        in_specs=in_specs,
        out_specs=pl.BlockSpec((tq, V_DIM), lambda b, hh, qi: (q_block0 + b * nq + qi, hh)),
        out_shape=jax.ShapeDtypeStruct((out_rows, nheads * V_DIM), BF16),
        input_output_aliases=aliases,
        compiler_params=_params("parallel", "parallel", "parallel"),
        name="attention",
    )(*args)


def _merge_kernel(h_ref, a_ref, gm_ref, cv_ref, wg0_ref, wg1_ref, wg2_ref, woa_ref, wob_ref, woc_ref, o_ref):
    h = h_ref[...]
    m = jax.nn.sigmoid(_dot(h, wg0_ref[...])) * _dot(a_ref[...], woa_ref[...])
    m = m + jax.nn.sigmoid(_dot(h, wg1_ref[...])) * _dot(gm_ref[...], wob_ref[...])
    m = m + jax.nn.sigmoid(_dot(h, wg2_ref[...])) * _dot(cv_ref[...], woc_ref[...])
    o_ref[...] = m.astype(BF16)


def _merge(nrows, lay, h, attn, gm, cv, w_all, w_oa, w_ob, w_oc):
    d = h.shape[1]
    tm, tn = CFG["tm_merge"], CFG["tn_merge"]
    ncol = d // tn
    gate0 = lay.block_index("gate")

    def act(width):
        return pl.BlockSpec((tm, width), lambda n, i: (i, 0))

    def gate_w(k):
        return pl.BlockSpec((d, tn), lambda n, i: (0, gate0 + k * ncol + n))

    def out_w(width):
        return pl.BlockSpec((width, tn), lambda n, i: (0, n))

    return pl.pallas_call(
        _merge_kernel,
        grid=(ncol, nrows // tm),
        in_specs=[act(d), act(attn.shape[1]), act(gm.shape[1]), act(cv.shape[1]),
                  gate_w(0), gate_w(1), gate_w(2),
                  out_w(w_oa.shape[0]), out_w(w_ob.shape[0]), out_w(w_oc.shape[0])],
        out_specs=pl.BlockSpec((tm, tn), lambda n, i: (i, n)),
        out_shape=jax.ShapeDtypeStruct((nrows, d), BF16),
        compiler_params=_params("parallel", "parallel"),
        name="merge_branches",
    )(h, attn, gm, cv, w_all, w_all, w_all, w_oa, w_ob, w_oc)


def _route(logits, carry, ne, ng):
    tm = logits.shape[0]
    neg = -1e30
    lane = lax.broadcasted_iota(jnp.int32, logits.shape, 1).astype(F32)
    big = float(4 * META_W)

    def first_lane(mask):
        return jnp.min(jnp.where(mask, lane, big), axis=-1, keepdims=True)

    gl = jnp.where((lane >= ne) & (lane < ne + ng), logits, neg)
    gmax = jnp.max(gl, axis=-1, keepdims=True)
    g_w = 1.0 / jnp.sum(jnp.exp(gl - gmax), axis=-1, keepdims=True)
    g_idx = first_lane(gl == gmax) - ne
    epg = ne // ng
    in_group = (lane >= g_idx * epg) & (lane < (g_idx + 1) * epg)
    el = jnp.where(in_group, logits, neg)
    ee = jnp.where(in_group, jnp.exp(el - jnp.max(el, axis=-1, keepdims=True)), -1.0)
    v1 = jnp.max(ee, axis=-1, keepdims=True)
    i1 = first_lane(ee == v1)
    ee2 = jnp.where(lane == i1, -1.0, ee)
    v2 = jnp.max(ee2, axis=-1, keepdims=True)
    i2 = first_lane(ee2 == v2)
    w1 = g_w * v1 / (v1 + v2)
    w2 = g_w * v2 / (v1 + v2)

    hit1 = lane == i1
    hit2 = lane == i2
    onehot = jnp.where(hit1 | hit2, 1.0, 0.0)
    r_i = lax.broadcasted_iota(jnp.int32, (tm, tm), 0)
    c_i = lax.broadcasted_iota(jnp.int32, (tm, tm), 1)
    earlier = jnp.where(r_i > c_i, 1.0, 0.0).astype(BF16)
    base = carry + _dot(earlier, onehot.astype(BF16))
    rank1 = jnp.sum(jnp.where(hit1, base, 0.0), axis=-1, keepdims=True)
    rank2 = jnp.sum(jnp.where(hit2, base, 0.0), axis=-1, keepdims=True)
    new_carry = carry + jnp.sum(onehot, axis=0, keepdims=True)

    rec = jnp.zeros(logits.shape, F32)
    for k, val in enumerate((i1, i2, rank1, rank2, w1, w2)):
        rec = jnp.where(lane == float(k), val, rec)
    return rec, new_carry


def _post1_kernel(mg_ref, xa_ref, xb_ref, wo_ref, gate_ref, sh_ref, sc_ref, lg_ref, lb_ref, wr_ref, br_ref,
                  x1_ref, t_ref, meta_ref, cnt_ref, carry_ref, *, nlat, alpha, ne, ng, sub):
    i = pl.program_id(0)

    @pl.when(i == 0)
    def _():
        carry_ref[...] = jnp.zeros(carry_ref.shape, F32)

    carry = carry_ref[0:1, :]
    blocks = [slice(r0, r0 + sub) for r0 in range(0, mg_ref.shape[0], sub)]
    ys = [_dot(mg_ref[rs, :], wo_ref[...]) for rs in blocks]
    for rs, y in zip(blocks, ys):
        x = jnp.where(i < nlat, xa_ref[rs, :], xb_ref[rs, :])
        x1 = _ln(alpha * x + gate_ref[...] * y) * lg_ref[...] + lb_ref[...]
        x1_ref[rs, :] = x1
        t = _ln(x1) * (1.0 + sc_ref[...]) + sh_ref[...]
        t_ref[rs, :] = _pack_rows(t)
        logits = _dot(t.astype(BF16), wr_ref[...]) + br_ref[...]
        rec, carry = _route(logits, carry, ne, ng)
        meta_ref[rs, :] = rec
    carry_ref[...] = jnp.broadcast_to(carry, carry_ref.shape)
    cnt_ref[...] = jnp.broadcast_to(carry, cnt_ref.shape)


def _post_mixer(rows, nblk, merged, xa, xb, w_o, m3, ln_g, ln_b, w_r, b_r, alpha, ne, ng):
    d = merged.shape[1]
    tm = rows.tm
    nrows = nblk * tm
    pw = _packed_width(d)
    return pl.pallas_call(
        functools.partial(_post1_kernel, nlat=rows.nlat, alpha=alpha, ne=ne, ng=ng, sub=min(tm, CFG["tm_post_sub"])),
        grid=(nblk,),
        in_specs=[_row_spec(tm, d)] + rows.two_source_specs(d) + [
            _const_spec((d, d)),
            rows.mod_spec(d, 2), rows.mod_spec(d, 3), rows.mod_spec(d, 4),
            _const_spec((1, d)), _const_spec((1, d)),
            _const_spec((d, META_W)), _const_spec((1, META_W)),
        ],
        out_specs=[_row_spec(tm, d), _row_spec(tm, pw), _row_spec(tm, META_W),
                   pl.BlockSpec((SUBLANES, META_W), lambda i: (0, 0))],
        out_shape=[
            jax.ShapeDtypeStruct((nrows, d), F32),
            jax.ShapeDtypeStruct((nrows, pw), jnp.uint32),
            jax.ShapeDtypeStruct((nrows, META_W), F32),
            jax.ShapeDtypeStruct((SUBLANES, META_W), F32),
        ],
        scratch_shapes=[pltpu.VMEM((SUBLANES, META_W), F32)],
        compiler_params=_params("arbitrary"),
        name="post_mixer_router",
    )(merged, xa, xb, w_o, m3, m3, m3, ln_g, ln_b, w_r, b_r)


ISSUE_UNROLL = 8


def _dispatch_kernel(dest_ref, tail_ref, t_ref, buf_ref, zero_ref, sem, zsem, *, tm, tme, ne):
    i = pl.program_id(0)

    @pl.when(i == 0)
    def _():
        zero_ref[...] = jnp.zeros(zero_ref.shape, zero_ref.dtype)

        def zero_copy(e):
            row0 = pl.multiple_of(jnp.maximum(tail_ref[e], 0) * tme, tme)
            return pltpu.make_async_copy(zero_ref, buf_ref.at[pl.ds(row0, tme)], zsem)

        for e in range(ne):
            @pl.when(tail_ref[e] >= 0)
            def _():
                zero_copy(e).start()
        for e in range(ne):
            @pl.when(tail_ref[e] >= 0)
            def _():
                zero_copy(e).wait()

    base = i * (2 * tm)
    for r in range(tm):
        for k in range(2):
            pltpu.make_async_copy(t_ref.at[pl.ds(r, 1)], buf_ref.at[pl.ds(dest_ref[base + 2 * r + k], 1)],
                                  sem).start(priority=k)
    for k in range(2):
        pltpu.make_async_copy(t_ref, buf_ref.at[pl.ds(0, tm)], sem).wait()


def _dispatch(dest, tail_blk, t_packed, buf_rows, tm, tme):
    nrows, pw = t_packed.shape
    return pl.pallas_call(
        functools.partial(_dispatch_kernel, tm=tm, tme=tme, ne=tail_blk.shape[0]),
        grid_spec=pltpu.PrefetchScalarGridSpec(
            num_scalar_prefetch=2,
            grid=(nrows // tm,),
            in_specs=[pl.BlockSpec((tm, pw), lambda i, dest, tail: (i, 0))],
            out_specs=pl.BlockSpec(memory_space=pl.ANY),
            scratch_shapes=[pltpu.VMEM((tme, pw), t_packed.dtype),
                            pltpu.SemaphoreType.DMA(()), pltpu.SemaphoreType.DMA(())],
        ),
        out_shape=jax.ShapeDtypeStruct((buf_rows, pw), t_packed.dtype),
        compiler_params=_params("arbitrary"),
        name="moe_dispatch",
    )(dest, tail_blk, t_packed)


def _expert_kernel(be_ref, nu_ref, nxt_ref, x_ref, w1_hbm, w3_hbm, w2_hbm, y_ref,
                   s1_ref, s3_ref, s2_ref, b1_ref, b3_ref, b2_ref, sem, *, layer):
    i = pl.program_id(0)

    def fetch(e):
        return (pltpu.make_async_copy(w1_hbm.at[layer, e], s1_ref, sem.at[0]),
                pltpu.make_async_copy(w3_hbm.at[layer, e], s3_ref, sem.at[1]),
                pltpu.make_async_copy(w2_hbm.at[layer, e], s2_ref, sem.at[2]))

    @pl.when(i < nu_ref[0])
    def _():
        e = be_ref[i]
        run_start = jnp.logical_or(i == 0, e != be_ref[jnp.maximum(i - 1, 0)])

        @pl.when(i == 0)
        def _():
            for cp in fetch(e):
                cp.start()

        @pl.when(run_start)
        def _():
            for cp in fetch(e):
                cp.wait()
            b1_ref[...] = s1_ref[...].astype(BF16)
            b3_ref[...] = s3_ref[...].astype(BF16)
            b2_ref[...] = s2_ref[...].astype(BF16)
            nxt = nxt_ref[i]

            @pl.when(nxt >= 0)
            def _():
                for cp in fetch(nxt):
                    cp.start()

        lo, hi = _unpack_rows(x_ref[...])
        half = lo.shape[1]

        def proj(b_ref):
            return _dot(lo, b_ref[:half, :]) + _dot(hi, b_ref[half:, :])

        a = proj(b1_ref)
        hid = (a * jax.nn.sigmoid(a) * proj(b3_ref)).astype(BF16)
        y_ref[...] = _dot(hid, b2_ref[...])


def _experts(block_e, n_used, next_e, buf, w1, w3, w2, layer):
    tme = CFG["tme"]
    nb = buf.shape[0] // tme
    pw = buf.shape[1]
    _, _, d, eh = w1.shape

    def row_block(i, be, nu, nxt):
        return (jnp.minimum(i, nu[0] - 1), 0)

    hbm = pl.BlockSpec(memory_space=pl.ANY)
    return pl.pallas_call(
        functools.partial(_expert_kernel, layer=layer),
        grid_spec=pltpu.PrefetchScalarGridSpec(
            num_scalar_prefetch=3,
            grid=(nb,),
            in_specs=[pl.BlockSpec((tme, pw), row_block), hbm, hbm, hbm],
            out_specs=pl.BlockSpec((tme, d), row_block),
            scratch_shapes=[pltpu.VMEM((d, eh), F32), pltpu.VMEM((d, eh), F32), pltpu.VMEM((eh, d), F32),
                            pltpu.VMEM((d, eh), BF16), pltpu.VMEM((d, eh), BF16), pltpu.VMEM((eh, d), BF16),
                            pltpu.SemaphoreType.DMA((3,))],
        ),
        out_shape=jax.ShapeDtypeStruct((nb * tme, d), F32),
        compiler_params=_params("arbitrary"),
        name="experts",
    )(block_e, n_used, next_e, buf, w1, w3, w2)


def _combine_kernel(dest_ref, y_ref, meta_ref, x1_ref, gate_ref, lg_ref, lb_ref, sh_ref, sc_ref, *rest,
                    tm, alpha, emit_h, chunk):
    if emit_h:
        x2_ref, h_ref, gbuf, sem = rest
    else:
        x2_ref, gbuf, sem = rest
    i = pl.program_id(0)
    nsteps = pl.num_programs(0)

    def start_row(blk, slot, r):
        a = (blk * tm + r) * 2
        for k in range(2):
            pltpu.make_async_copy(y_ref.at[pl.ds(dest_ref[a + k], 1)], gbuf.at[slot, k, pl.ds(r, 1)],
                                  sem.at[slot]).start()

    def wait_slot(slot):
        for k in range(2):
            pltpu.make_async_copy(y_ref.at[pl.ds(0, tm)], gbuf.at[slot, k], sem.at[slot]).wait()

    @pl.when(i == 0)
    def _():
        def body(r, carry):
            start_row(0, 0, r)
            return carry
        lax.fori_loop(0, tm, body, 0, unroll=ISSUE_UNROLL)

    slot = i % 2
    wait_slot(slot)

    nxt = jnp.minimum(i + 1, nsteps - 1)
    for r0 in range(0, tm, chunk):
        for r in range(r0, r0 + chunk):
            start_row(nxt, 1 - slot, r)
        rs = slice(r0, r0 + chunk)
        meta = meta_ref[rs, :]
        f = meta[:, 4:5] * gbuf[slot, 0, rs, :] + meta[:, 5:6] * gbuf[slot, 1, rs, :]
        x2 = _ln(alpha * x1_ref[rs, :] + gate_ref[...] * f) * lg_ref[...] + lb_ref[...]
        x2_ref[rs, :] = x2
        if emit_h:
            h_ref[rs, :] = (_ln(x2) * (1.0 + sc_ref[...]) + sh_ref[...]).astype(BF16)

    @pl.when(i == nsteps - 1)
    def _():
        wait_slot(1 - slot)


def _combine(rows, nblk, dest, ybuf, meta, x1, m3, ln_g, ln_b, m3_next, alpha):
    d = x1.shape[1]
    tm = rows.tm
    nrows = nblk * tm
    emit_h = m3_next is not None
    if not emit_h:
        m3_next = m3

    def rspec(width):
        return pl.BlockSpec((tm, width), lambda i, dest: (i, 0))

    def cspec(shape):
        return pl.BlockSpec(shape, lambda i, dest: (0,) * len(shape))

    def mspec(seg):
        return pl.BlockSpec((None, 1, d), lambda i, dest: (rows.mod_row(i), 0, seg))

    out_specs = [rspec(d)]
    out_shape = [jax.ShapeDtypeStruct((nrows, d), F32)]
    if emit_h:
        out_specs.append(rspec(d))
        out_shape.append(jax.ShapeDtypeStruct((nrows, d), BF16))
    out = pl.pallas_call(
        functools.partial(_combine_kernel, tm=tm, alpha=alpha, emit_h=emit_h, chunk=min(tm, CFG["combine_chunk"])),
        grid_spec=pltpu.PrefetchScalarGridSpec(
            num_scalar_prefetch=1,
            grid=(nblk,),
            in_specs=[pl.BlockSpec(memory_space=pl.ANY), rspec(META_W), rspec(d), mspec(5),
                      cspec((1, d)), cspec((1, d)), mspec(0), mspec(1)],
            out_specs=out_specs,
            scratch_shapes=[pltpu.VMEM((2, 2, tm, d), F32), pltpu.SemaphoreType.DMA((2,))],
        ),
        out_shape=out_shape,
        compiler_params=_params("arbitrary"),
        name="moe_combine",
    )(dest, ybuf, meta, x1, m3, ln_g, ln_b, m3_next, m3_next)
    return out if emit_h else (out[0], None)


def _rot_cols(w):
    q = ROPE_DIM // 4
    return jnp.concatenate([-w[..., q:2 * q], w[..., 0:q], -w[..., 3 * q:4 * q], w[..., 2 * q:3 * q]], axis=-1)


KIND_COPY, KIND_ROPE, KIND_ZERO = 0, 1, 2


def _reorder_kernel(ia_ref, ib_ref, kind_ref, a_ref, b_ref, o_ref):
    del ia_ref, ib_ref
    kind = kind_ref[pl.program_id(0)]
    a = a_ref[...]

    @pl.when(kind == KIND_COPY)
    def _():
        o_ref[...] = jnp.concatenate([a, b_ref[...]], axis=0).T.astype(BF16)

    @pl.when(kind == KIND_ROPE)
    def _():
        q = ROPE_DIM // 4
        rot = jnp.concatenate([-a[q:2 * q], a[0:q], -a[3 * q:4 * q], a[2 * q:3 * q]], axis=0)
        o_ref[...] = jnp.concatenate([a, rot], axis=0).T.astype(BF16)

    @pl.when(kind == KIND_ZERO)
    def _():
        o_ref[...] = jnp.zeros(o_ref.shape, BF16)


def _reorder_w_in(w_in, layer, lay, ql, kvl, gwidth, cwidth):
    _, d, p_in = w_in.shape
    piece = ROPE_DIM
    off_kr = ql + kvl
    off_gm = off_kr + ROPE_DIM
    off_cv = off_gm + 2 * gwidth
    off_gate = off_cv + 3 * cwidth
    src_of = {"qa": 0, "ckv": ql, "gm": off_gm, "cb": off_cv, "cc": off_cv + cwidth, "cx": off_cv + 2 * cwidth,
              "gate": off_gate, "kr": off_kr}
    nblk = lay.width // LANES
    ia = [0] * nblk
    ib = [0] * nblk
    kind = [KIND_ZERO] * nblk
    for name, (off, width, _) in lay.segs.items():
        assert off % LANES == 0 and src_of[name] % piece == 0
        for j in range(width // LANES):
            src = src_of[name] + j * LANES
            ia[off // LANES + j] = src // piece
            ib[off // LANES + j] = src // piece if name == "kr" else src // piece + 1
            kind[off // LANES + j] = KIND_ROPE if name == "kr" else KIND_COPY
    tables = [jnp.asarray(t, dtype=jnp.int32) for t in (ia, ib, kind)]
    return pl.pallas_call(
        _reorder_kernel,
        grid_spec=pltpu.PrefetchScalarGridSpec(
            num_scalar_prefetch=3,
            grid=(nblk,),
            in_specs=[pl.BlockSpec((None, piece, d), lambda j, ia, ib, kind: (layer, ia[j], 0)),
                      pl.BlockSpec((None, piece, d), lambda j, ia, ib, kind: (layer, ib[j], 0))],
            out_specs=pl.BlockSpec((d, LANES), lambda j, ia, ib, kind: (0, j)),
        ),
        out_shape=jax.ShapeDtypeStruct((d, lay.width), BF16),
        compiler_params=_params("arbitrary"),
        name="reorder_w_in",
    )(*tables, *([jnp.swapaxes(w_in, 1, 2)] * 2))


def _rope_table(seq, tm):
    n_rows = seq // GRID_W
    t = jnp.arange(n_rows * GRID_W)
    row = (t // GRID_W).astype(F32)
    col = (t % GRID_W).astype(F32)
    half = ROPE_DIM // 2
    inv = ROPE_BASE ** (-jnp.arange(0, half, 2, dtype=F32) / half)
    ang = jnp.concatenate([row[:, None] * inv] * 2 + [col[:, None] * inv] * 2, axis=-1)
    cs = jnp.concatenate([jnp.cos(ang), jnp.sin(ang)], axis=-1)
    ident = jnp.concatenate([jnp.ones((tm, ROPE_DIM), F32), jnp.zeros((tm, ROPE_DIM), F32)], axis=-1)
    return jnp.concatenate([cs, ident], axis=0)


def kernel(x, c, ctx, c_ctx, w_mod, b_mod, w_in, g_q, w_uq, g_kv, w_ukv, w_oa, gm_ln_g, gm_ln_b, w_s, b_s, w_ob,
           w_conv, w_oc, w_o, ln1_g, ln1_b, w_group, b_group, w_expert, b_expert, w1, w3, w2, ln2_g, ln2_b):
    bsz, seq, d = x.shape
    ctx_len = ctx.shape[1]
    depth = w_mod.shape[0]
    ql = g_q.shape[1]
    kvl = g_kv.shape[1]
    nheads = w_uq.shape[2] // (NOPE_DIM + ROPE_DIM)
    gwidth = gm_ln_g.shape[1]
    cwidth = w_conv.shape[2]
    ne = w_expert.shape[2]
    ng = w_group.shape[2]
    alpha = float((2 * depth) ** 0.25)
    tm, tq, tme = CFG["tm"], CFG["tq"], CFG["tme"]
    rows = _Rows(bsz, seq, ctx_len, tm)
    rows_post = _Rows(bsz, seq, ctx_len, CFG["tm_post"], ctx_blocks_may_span=True)
    lay = _InLayout(d, ql, kvl, gwidth, cwidth, CFG["tn_merge"])
    n_lat, n_ctx = bsz * seq, bsz * ctx_len

    mod_rows = -(-(bsz + 1) // SUBLANES) * SUBLANES
    cc = jnp.zeros((mod_rows, d), F32).at[:bsz].set(c).at[bsz].set(c_ctx)
    m_all = _mod_vectors(cc, w_mod, b_mod)
    cs = _rope_table(seq, tm)

    xa = x.reshape(n_lat, d)
    xb = ctx.reshape(n_ctx, d)
    h = None
    for l in range(depth):
        last = l == depth - 1
        m3 = m_all[l].reshape(mod_rows, 1, 6 * d)
        nblk = rows.nlat if last else rows.nall
        nrows = nblk * tm

        w_all = _reorder_w_in(w_in, l, lay, ql, kvl, gwidth, cwidth)
        uq = w_uq[l].reshape(ql, nheads, NOPE_DIM + ROPE_DIM)
        uq_rope = uq[..., NOPE_DIM:]
        w_uq_l = jnp.concatenate([uq, _rot_cols(uq_rope)], axis=-1).reshape(ql, nheads * HEAD_W).astype(BF16)
        ukv = w_ukv[l].reshape(kvl, nheads, NOPE_DIM + V_DIM)
        w_uk = ukv[..., :NOPE_DIM].reshape(kvl, nheads * NOPE_DIM).astype(BF16)
        w_uv = ukv[..., NOPE_DIM:].reshape(kvl, nheads * V_DIM).astype(BF16)
        bs_exp = jnp.broadcast_to(b_s[l][:, :, None], b_s.shape[1:] + (gwidth // w_s.shape[1],))
        w_r = jnp.zeros((d, META_W), F32).at[:, :ne].set(w_expert[l]).at[:, ne:ne + ng].set(w_group[l]).astype(BF16)
        b_r = jnp.zeros((1, META_W), F32).at[0, :ne].set(b_expert[l]).at[0, ne:ne + ng].set(b_group[l])

        if l == 0:
            h = _ln_modulate(rows, xa, xb, m3)

        q, kcat, vcat, gm, gate_b, y = _mixer_in(rows, nblk, lay, h, w_all, g_q[l][None], w_uq_l, g_kv[l][None],
                                                 w_uk, w_uv, gm_ln_g[l][None], gm_ln_b[l][None],
                                                 w_s[l].astype(BF16), bs_exp, cs, nheads)
        cv = _conv_gate(rows, nblk, gate_b, y, w_conv[l])
        lk = ctx_len + seq
        attn = _attention(q, kcat, vcat, bsz, nheads, seq, min(tq, seq), 0, lk, 1, nrows)
        if not last:
            assert lk % ctx_len == 0
            tqc = min(tq, ctx_len)
            attn = _attention(q, kcat, vcat, bsz, nheads, ctx_len, tqc, n_lat // tqc, ctx_len, lk // ctx_len, nrows,
                              into=attn)
        merged = _merge(nrows, lay, h, attn, gm, cv, w_all, w_oa[l].astype(BF16), w_ob[l].astype(BF16),
                        w_oc[l].astype(BF16))
        x1, t_packed, meta, cnt = _post_mixer(rows_post, nrows // rows_post.tm, merged, xa, xb, w_o[l].astype(BF16),
                                              m3, ln1_g[l][None], ln1_b[l][None], w_r, b_r, alpha, ne, ng)

        e_ids = meta[:, 0:2].astype(jnp.int32)
        ranks = meta[:, 2:4].astype(jnp.int32)
        counts = cnt[0, :ne].astype(jnp.int32)
        experts = jnp.arange(ne, dtype=jnp.int32)
        padded = (counts + tme - 1) // tme * tme
        pend = jnp.cumsum(padded)
        pstart = pend - padded
        dest = (jnp.sum(jnp.where(e_ids[..., None] == experts, pstart, 0), axis=-1) + ranks).reshape(-1)
        nb = 2 * nrows // tme + ne
        n_used = pend[-1:] // tme
        blk = jnp.minimum(jnp.arange(nb, dtype=jnp.int32), n_used[0] - 1)
        block_e = jnp.minimum(jnp.sum(pend[None, :] <= (blk * tme)[:, None], axis=1), ne - 1).astype(jnp.int32)
        tail_blk = jnp.where(padded > 0, pend // tme - 1, -1).astype(jnp.int32)
        nonempty_from = lax.cummin(jnp.where(padded > 0, experts, ne), axis=0, reverse=True)
        next_nonempty = jnp.concatenate([nonempty_from[1:], jnp.full((1,), ne, jnp.int32)])
        next_e = next_nonempty[block_e]
        next_e = jnp.where(next_e >= ne, -1, next_e).astype(jnp.int32)

        buf = _dispatch(dest, tail_blk, t_packed, nb * tme, tm, tme)
        ybuf = _experts(block_e, n_used.astype(jnp.int32), next_e, buf, w1, w3, w2, l)
        m3_next = None if last else m_all[l + 1].reshape(mod_rows, 1, 6 * d)
        x2, h = _combine(rows, nblk, dest, ybuf, meta, x1, m3, ln2_g[l][None], ln2_b[l][None], m3_next, alpha)
        xa, xb = x2, x2
    return xa.reshape(bsz, seq, d)
```

```python
import functools

import jax
import jax.numpy as jnp
from jax import lax
from jax.experimental import pallas as pl
from jax.experimental.pallas import tpu as pltpu

F32 = jnp.float32
BF16 = jnp.bfloat16

GRID_W = 64
NOPE_DIM = 128
ROPE_DIM = 64
V_DIM = 128
ROPE_BASE = 10000.0
LN_EPS = 1e-6
HEAD_W = NOPE_DIM + 2 * ROPE_DIM
META_W = 128
LANES = 128
SUBLANES = 8
LOG2_E = 1.4426950408889634

CFG = dict(
    tm=256,
    tm_post=512,
    tm_post_sub=256,
    tq=2048,
    tq_sub=256,
    tm_merge=512,
    tn_merge=512,
    tn_mod=1024,
    tme=256,
    combine_chunk=32,
)
VMEM_LIMIT = 56 * 1024 * 1024


def _params(*sem):
    return pltpu.CompilerParams(dimension_semantics=sem, vmem_limit_bytes=VMEM_LIMIT)


def _dot(a, b):
    return jnp.dot(a, b, preferred_element_type=F32)


def _ln(x):
    mu = jnp.mean(x, axis=-1, keepdims=True)
    xc = x - mu
    var = jnp.mean(xc * xc, axis=-1, keepdims=True)
    return xc * lax.rsqrt(var + LN_EPS)


def _rms(x):
    return x * lax.rsqrt(jnp.mean(x * x, axis=-1, keepdims=True) + LN_EPS)


def _pack_rows(t):
    half = t.shape[1] // 2
    return pltpu.pack_elementwise([t[:, :half], t[:, half:]], packed_dtype=BF16)


def _unpack_rows(xp):
    lo = pltpu.unpack_elementwise(xp, index=0, packed_dtype=BF16, unpacked_dtype=F32)
    hi = pltpu.unpack_elementwise(xp, index=1, packed_dtype=BF16, unpacked_dtype=F32)
    return lo.astype(BF16), hi.astype(BF16)


def _packed_width(d):
    return d // 2


def _mod_kernel(c_ref, w_ref, b_ref, o_ref):
    c = c_ref[...]
    a = (c * jax.nn.sigmoid(c)).astype(BF16)
    o_ref[...] = _dot(a, w_ref[...].astype(BF16)) + b_ref[...]


def _mod_vectors(cc, w_mod, b_mod):
    nl, d, d6 = w_mod.shape
    rows = cc.shape[0]
    tn = CFG["tn_mod"]
    return pl.pallas_call(
        _mod_kernel,
        grid=(nl, d6 // tn),
        in_specs=[
            pl.BlockSpec((rows, d), lambda l, n: (0, 0)),
            pl.BlockSpec((None, d, tn), lambda l, n: (l, 0, n)),
            pl.BlockSpec((None, 1, tn), lambda l, n: (l, 0, n)),
        ],
        out_specs=pl.BlockSpec((None, rows, tn), lambda l, n: (l, 0, n)),
        out_shape=jax.ShapeDtypeStruct((nl, rows, d6), F32),
        compiler_params=_params("parallel", "parallel"),
        name="mod_vectors",
    )(cc, w_mod, b_mod.reshape(nl, 1, d6))


class _Rows:
    def __init__(self, bsz, seq, ctx_len, tm, ctx_blocks_may_span=False):
        assert seq % tm == 0 and (bsz * ctx_len) % tm == 0
        assert ctx_blocks_may_span or ctx_len % tm == 0
        self.bsz, self.seq, self.ctx_len, self.tm = bsz, seq, ctx_len, tm
        self.nsb = seq // tm
        self.ncb = ctx_len // tm
        self.nlat = bsz * self.nsb
        self.nctx = bsz * ctx_len // tm
        self.nall = self.nlat + self.nctx

    def mod_row(self, i):
        return jnp.where(i < self.nlat, i // self.nsb, self.bsz)

    def mod_spec(self, d, seg):
        return pl.BlockSpec((None, 1, d), lambda i: (self.mod_row(i), 0, seg))

    def rope_spec(self):
        return pl.BlockSpec((self.tm, 2 * ROPE_DIM), lambda i: (jnp.where(i < self.nlat, i % self.nsb, self.nsb), 0))

    def two_source_specs(self, d, ctx_buffers=2):
        return [
            pl.BlockSpec((self.tm, d), lambda i: (jnp.minimum(i, self.nlat - 1), 0)),
            pl.BlockSpec((self.tm, d), lambda i: (jnp.maximum(i - self.nlat, 0), 0),
                         pipeline_mode=pl.Buffered(ctx_buffers)),
        ]

    def seq_edges(self, i):
        j = i - self.nlat
        first = jnp.where(i < self.nlat, i % self.nsb == 0, j % self.ncb == 0)
        last = jnp.where(i < self.nlat, i % self.nsb == self.nsb - 1, j % self.ncb == self.ncb - 1)
        return first, last


def _row_spec(tm, width):
    return pl.BlockSpec((tm, width), lambda i: (i, 0))


def _const_spec(shape):
    return pl.BlockSpec(shape, lambda i: (0,) * len(shape), pipeline_mode=pl.Buffered(1))


def _lnmod_kernel(xa_ref, xb_ref, sh_ref, sc_ref, h_ref, *, nlat):
    x = jnp.where(pl.program_id(0) < nlat, xa_ref[...], xb_ref[...])
    h_ref[...] = (_ln(x) * (1.0 + sc_ref[...]) + sh_ref[...]).astype(BF16)


def _ln_modulate(rows, xa, xb, m3):
    d = xa.shape[1]
    return pl.pallas_call(
        functools.partial(_lnmod_kernel, nlat=rows.nlat),
        grid=(rows.nall,),
        in_specs=rows.two_source_specs(d) + [rows.mod_spec(d, 0), rows.mod_spec(d, 1)],
        out_specs=_row_spec(rows.tm, d),
        out_shape=jax.ShapeDtypeStruct((rows.nall * rows.tm, d), BF16),
        compiler_params=_params("parallel"),
        name="ln_modulate",
    )(xa, xb, m3, m3)


class _InLayout:
    def __init__(self, d, ql, kvl, gwidth, cwidth, tn_gate):
        self.segs = {}
        off = 0
        for name, width, block in (("qa", ql, ql), ("ckv", kvl, kvl), ("gm", 2 * gwidth, 2 * gwidth),
                                   ("cb", cwidth, cwidth), ("cc", cwidth, cwidth), ("cx", cwidth, cwidth),
                                   ("gate", 3 * d, tn_gate), ("kr", 2 * ROPE_DIM, 2 * ROPE_DIM)):
            off = -(-off // block) * block
            self.segs[name] = (off, width, block)
            off += width
        self.width = -(-off // LANES) * LANES

    def block_index(self, name):
        off, _, block = self.segs[name]
        return off // block

    def spec(self, d, name):
        _, width, block = self.segs[name]
        assert width == block
        idx = self.block_index(name)
        return pl.BlockSpec((d, width), lambda i: (0, idx), pipeline_mode=pl.Buffered(1))


def _rope(x2, cs):
    w = x2 * cs
    return w + pltpu.roll(w, ROPE_DIM, axis=1)


def _mixer_in_kernel(h_ref, wqa_ref, wckv_ref, wkr_ref, wgm_ref, wcb_ref, wcc_ref, wcx_ref,
                     gq_ref, wuq_ref, gkv_ref, wuk_ref, wuv_ref, lg_ref, lb_ref, ws_ref, bs_ref, cs_ref,
                     q_ref, k_ref, v_ref, gm_ref, gb_ref, y_ref, *, nheads, nfull, width, chunk, groups):
    h = h_ref[...]
    cs = cs_ref[...]

    c = (_rms(_dot(h, wckv_ref[...])) * gkv_ref[...]).astype(BF16)
    kr = _rope(_dot(h, wkr_ref[...]), cs)
    lane = lax.broadcasted_iota(jnp.int32, kr.shape, 1)
    kr = jnp.where(lane < ROPE_DIM, kr, 0.0).astype(BF16)
    kn = _dot(c, wuk_ref[...]).astype(BF16)
    v_ref[...] = _dot(c, wuv_ref[...]).astype(BF16)
    for hh in range(nheads):
        k_ref[:, hh * HEAD_W:hh * HEAD_W + NOPE_DIM] = kn[:, hh * NOPE_DIM:(hh + 1) * NOPE_DIM]
        k_ref[:, hh * HEAD_W + NOPE_DIM:(hh + 1) * HEAD_W] = kr

    def rest():
        a = _rms(_dot(h, wqa_ref[...])) * gq_ref[...]
        q = _dot(a.astype(BF16), wuq_ref[...])
        for hh in range(nheads):
            base = hh * HEAD_W
            q_ref[:, base:base + NOPE_DIM] = q[:, base:base + NOPE_DIM].astype(BF16)
            q_ref[:, base + NOPE_DIM:base + HEAD_W] = _rope(q[:, base + NOPE_DIM:base + HEAD_W], cs).astype(BF16)
        z = jax.nn.gelu(_dot(h, wgm_ref[...]))
        u = z[:, :width]
        vv = (_ln(z[:, width:]) * lg_ref[...] + lb_ref[...]).astype(BF16)
        gw = width // groups
        for ci in range(z.shape[0] // chunk):
            r0 = ci * chunk
            for g in range(groups):
                c0 = g * gw
                mixed = _dot(ws_ref[g], vv[r0:r0 + chunk, c0:c0 + gw]) + bs_ref[g]
                gm_ref[r0:r0 + chunk, c0:c0 + gw] = (u[r0:r0 + chunk, c0:c0 + gw] * mixed).astype(BF16)
        gb_ref[...] = _dot(h, wcb_ref[...])
        y_ref[...] = _dot(h, wcc_ref[...]) * _dot(h, wcx_ref[...])

    if nfull is None:
        rest()
    else:
        pl.when(pl.program_id(0) < nfull)(rest)


def _mixer_in(rows, nblk, lay, h, w_all, g_q, w_uq, g_kv, w_uk, w_uv, ln_g, ln_b, w_s, bs_exp, cs, nheads):
    d = h.shape[1]
    tm = rows.tm
    ql = lay.segs["qa"][1]
    kvl = lay.segs["ckv"][1]
    width = lay.segs["gm"][1] // 2
    cw = lay.segs["cb"][1]
    groups, chunk, _ = w_s.shape
    assert tm % chunk == 0
    per_batch = rows.ncb + rows.nsb
    nrows = nblk * tm

    def kv_block(i):
        j = i - rows.nlat
        lat = (i // rows.nsb) * per_batch + rows.ncb + i % rows.nsb
        ctx = (j // rows.ncb) * per_batch + j % rows.ncb
        return jnp.where(i < rows.nlat, lat, ctx)

    def full_spec(w):
        return pl.BlockSpec((tm, w), lambda i: (jnp.minimum(i, nblk - 1), 0))

    def kv_spec(w):
        return pl.BlockSpec((tm, w), lambda i: (kv_block(i), 0))

    return pl.pallas_call(
        functools.partial(_mixer_in_kernel, nheads=nheads, nfull=None if nblk == rows.nall else nblk,
                          width=width, chunk=chunk, groups=groups),
        grid=(rows.nall,),
        in_specs=[_row_spec(tm, d)]
        + [lay.spec(d, name) for name in ("qa", "ckv", "kr", "gm", "cb", "cc", "cx")]
        + [_const_spec((1, ql)), _const_spec((ql, nheads * HEAD_W)),
           _const_spec((1, kvl)), _const_spec((kvl, nheads * NOPE_DIM)), _const_spec((kvl, nheads * V_DIM)),
           _const_spec((1, width)), _const_spec((1, width)),
           _const_spec((groups, chunk, chunk)), _const_spec((groups, chunk, width // groups)),
           rows.rope_spec()],
        out_specs=[full_spec(nheads * HEAD_W), kv_spec(nheads * HEAD_W), kv_spec(nheads * V_DIM),
                   full_spec(width), full_spec(cw), full_spec(cw)],
        out_shape=[
            jax.ShapeDtypeStruct((nrows, nheads * HEAD_W), BF16),
            jax.ShapeDtypeStruct((rows.nall * tm, nheads * HEAD_W), BF16),
            jax.ShapeDtypeStruct((rows.nall * tm, nheads * V_DIM), BF16),
            jax.ShapeDtypeStruct((nrows, width), BF16),
            jax.ShapeDtypeStruct((nrows, cw), F32),
            jax.ShapeDtypeStruct((nrows, cw), F32),
        ],
        compiler_params=_params("arbitrary"),
        name="mixer_in",
    )(h, *([w_all] * 7), g_q, w_uq, g_kv, w_uk, w_uv, ln_g, ln_b, w_s, bs_exp, cs)


def _conv_gate_kernel(gb_ref, y_ref, yp_ref, yn_ref, wc_ref, o_ref, *, rows):
    first, last = rows.seq_edges(pl.program_id(0))
    y = y_ref[...]
    y_prev = jnp.where(first, 0.0, yp_ref[SUBLANES - 1:SUBLANES, :])
    y_next = jnp.where(last, 0.0, yn_ref[0:1, :])
    tm = y.shape[0]
    row = lax.broadcasted_iota(jnp.int32, y.shape, 0)
    y_dn = jnp.where(row == 0, y_prev, pltpu.roll(y, 1, axis=0))
    y_up = jnp.where(row == tm - 1, y_next, pltpu.roll(y, tm - 1, axis=0))
    wc = wc_ref[...]
    conv = y_dn * wc[0:1, :] + y * wc[1:2, :] + y_up * wc[2:3, :]
    o_ref[...] = (gb_ref[...] * conv).astype(BF16)


def _conv_gate(rows, nblk, gate_b, y, w_conv):
    cw = y.shape[1]
    per = rows.tm // SUBLANES
    nhalo = y.shape[0] // SUBLANES
    return pl.pallas_call(
        functools.partial(_conv_gate_kernel, rows=rows),
        grid=(nblk,),
        in_specs=[
            _row_spec(rows.tm, cw),
            _row_spec(rows.tm, cw),
            pl.BlockSpec((SUBLANES, cw), lambda i: (jnp.maximum(i * per - 1, 0), 0)),
            pl.BlockSpec((SUBLANES, cw), lambda i: (jnp.minimum((i + 1) * per, nhalo - 1), 0)),
            _const_spec(w_conv.shape),
        ],
        out_specs=_row_spec(rows.tm, cw),
        out_shape=jax.ShapeDtypeStruct((nblk * rows.tm, cw), BF16),
        compiler_params=_params("parallel"),
        name="conv_gate",
    )(gate_b, y, y, y, w_conv)


def _attn_kernel(q_ref, k_ref, v_ref, *rest, scale, sub):
    o_ref = rest[-1]
    k = k_ref[...]
    v = v_ref[...]
    starts = list(range(0, q_ref.shape[0], sub))

    def scores(r0):
        return lax.dot_general(q_ref[r0:r0 + sub, :], k, (((1,), (1,)), ((), ())), preferred_element_type=F32)

    def weights(s):
        p = jnp.exp2((s - jnp.max(s, axis=-1, keepdims=True)) * (scale * LOG2_E))
        return p.astype(BF16), jnp.sum(p, axis=-1, keepdims=True)

    def emit(r0, p, denom):
        o_ref[r0:r0 + sub, :] = (_dot(p, v) / denom).astype(BF16)

    s_next = scores(starts[0])
    pending = None
    for c, r0 in enumerate(starts):
        s = s_next
        if c + 1 < len(starts):
            s_next = scores(starts[c + 1])
        current = (r0,) + weights(s)
        if pending is not None:
            emit(*pending)
        pending = current
    emit(*pending)


def _attention(q, k, v, bsz, nheads, lq, tq, q_block0, lk, k_block_stride, out_rows, into=None):
    nq = lq // tq
    in_specs = [
        pl.BlockSpec((tq, HEAD_W), lambda b, hh, qi: (q_block0 + b * nq + qi, hh)),
        pl.BlockSpec((lk, HEAD_W), lambda b, hh, qi: (b * k_block_stride, hh)),
        pl.BlockSpec((lk, V_DIM), lambda b, hh, qi: (b * k_block_stride, hh)),
    ]
    args = [q, k, v]
    aliases = {}
    if into is not None:
        in_specs.append(pl.BlockSpec(memory_space=pl.ANY))
        args.append(into)
        aliases = {3: 0}
    return pl.pallas_call(
        functools.partial(_attn_kernel, scale=float(NOPE_DIM + ROPE_DIM) ** -0.5, sub=min(tq, CFG["tq_sub"])),
        grid=(bsz, nheads, nq),
        in_specs=in_specs,
        out_specs=pl.BlockSpec((tq, V_DIM), lambda b, hh, qi: (q_block0 + b * nq + qi, hh)),
        out_shape=jax.ShapeDtypeStruct((out_rows, nheads * V_DIM), BF16),
        input_output_aliases=aliases,
        compiler_params=_params("parallel", "parallel", "parallel"),
        name="attention",
    )(*args)


def _merge_kernel(h_ref, a_ref, gm_ref, cv_ref, wg0_ref, wg1_ref, wg2_ref, woa_ref, wob_ref, woc_ref, o_ref):
    h = h_ref[...]
    m = jax.nn.sigmoid(_dot(h, wg0_ref[...])) * _dot(a_ref[...], woa_ref[...])
    m = m + jax.nn.sigmoid(_dot(h, wg1_ref[...])) * _dot(gm_ref[...], wob_ref[...])
    m = m + jax.nn.sigmoid(_dot(h, wg2_ref[...])) * _dot(cv_ref[...], woc_ref[...])
    o_ref[...] = m.astype(BF16)


def _merge(nrows, lay, h, attn, gm, cv, w_all, w_oa, w_ob, w_oc):
    d = h.shape[1]
    tm, tn = CFG["tm_merge"], CFG["tn_merge"]
    ncol = d // tn
    gate0 = lay.block_index("gate")

    def act(width):
        return pl.BlockSpec((tm, width), lambda n, i: (i, 0))

    def gate_w(k):
        return pl.BlockSpec((d, tn), lambda n, i: (0, gate0 + k * ncol + n))

    def out_w(width):
        return pl.BlockSpec((width, tn), lambda n, i: (0, n))

    return pl.pallas_call(
        _merge_kernel,
        grid=(ncol, nrows // tm),
        in_specs=[act(d), act(attn.shape[1]), act(gm.shape[1]), act(cv.shape[1]),
                  gate_w(0), gate_w(1), gate_w(2),
                  out_w(w_oa.shape[0]), out_w(w_ob.shape[0]), out_w(w_oc.shape[0])],
        out_specs=pl.BlockSpec((tm, tn), lambda n, i: (i, n)),
        out_shape=jax.ShapeDtypeStruct((nrows, d), BF16),
        compiler_params=_params("parallel", "parallel"),
        name="merge_branches",
    )(h, attn, gm, cv, w_all, w_all, w_all, w_oa, w_ob, w_oc)


def _route(logits, carry, ne, ng):
    tm = logits.shape[0]
    neg = -1e30
    lane = lax.broadcasted_iota(jnp.int32, logits.shape, 1).astype(F32)
    big = float(4 * META_W)

    def first_lane(mask):
        return jnp.min(jnp.where(mask, lane, big), axis=-1, keepdims=True)

    gl = jnp.where((lane >= ne) & (lane < ne + ng), logits, neg)
    gmax = jnp.max(gl, axis=-1, keepdims=True)
    g_w = 1.0 / jnp.sum(jnp.exp(gl - gmax), axis=-1, keepdims=True)
    g_idx = first_lane(gl == gmax) - ne
    epg = ne // ng
    in_group = (lane >= g_idx * epg) & (lane < (g_idx + 1) * epg)
    el = jnp.where(in_group, logits, neg)
    ee = jnp.where(in_group, jnp.exp(el - jnp.max(el, axis=-1, keepdims=True)), -1.0)
    v1 = jnp.max(ee, axis=-1, keepdims=True)
    i1 = first_lane(ee == v1)
    ee2 = jnp.where(lane == i1, -1.0, ee)
    v2 = jnp.max(ee2, axis=-1, keepdims=True)
    i2 = first_lane(ee2 == v2)
    w1 = g_w * v1 / (v1 + v2)
    w2 = g_w * v2 / (v1 + v2)

    hit1 = lane == i1
    hit2 = lane == i2
    onehot = jnp.where(hit1 | hit2, 1.0, 0.0)
    r_i = lax.broadcasted_iota(jnp.int32, (tm, tm), 0)
    c_i = lax.broadcasted_iota(jnp.int32, (tm, tm), 1)
    earlier = jnp.where(r_i > c_i, 1.0, 0.0).astype(BF16)
    base = carry + _dot(earlier, onehot.astype(BF16))
    rank1 = jnp.sum(jnp.where(hit1, base, 0.0), axis=-1, keepdims=True)
    rank2 = jnp.sum(jnp.where(hit2, base, 0.0), axis=-1, keepdims=True)
    new_carry = carry + jnp.sum(onehot, axis=0, keepdims=True)

    rec = jnp.zeros(logits.shape, F32)
    for k, val in enumerate((i1, i2, rank1, rank2, w1, w2)):
        rec = jnp.where(lane == float(k), val, rec)
    return rec, new_carry


def _post1_kernel(mg_ref, xa_ref, xb_ref, wo_ref, gate_ref, sh_ref, sc_ref, lg_ref, lb_ref, wr_ref, br_ref,
                  x1_ref, t_ref, meta_ref, cnt_ref, carry_ref, *, nlat, alpha, ne, ng, sub):
    i = pl.program_id(0)

    @pl.when(i == 0)
    def _():
        carry_ref[...] = jnp.zeros(carry_ref.shape, F32)

    carry = carry_ref[0:1, :]
    blocks = [slice(r0, r0 + sub) for r0 in range(0, mg_ref.shape[0], sub)]
    ys = [_dot(mg_ref[rs, :], wo_ref[...]) for rs in blocks]
    for rs, y in zip(blocks, ys):
        x = jnp.where(i < nlat, xa_ref[rs, :], xb_ref[rs, :])
        x1 = _ln(alpha * x + gate_ref[...] * y) * lg_ref[...] + lb_ref[...]
        x1_ref[rs, :] = x1
        t = _ln(x1) * (1.0 + sc_ref[...]) + sh_ref[...]
        t_ref[rs, :] = _pack_rows(t)
        logits = _dot(t.astype(BF16), wr_ref[...]) + br_ref[...]
        rec, carry = _route(logits, carry, ne, ng)
        meta_ref[rs, :] = rec
    carry_ref[...] = jnp.broadcast_to(carry, carry_ref.shape)
    cnt_ref[...] = jnp.broadcast_to(carry, cnt_ref.shape)


def _post_mixer(rows, nblk, merged, xa, xb, w_o, m3, ln_g, ln_b, w_r, b_r, alpha, ne, ng):
    d = merged.shape[1]
    tm = rows.tm
    nrows = nblk * tm
    pw = _packed_width(d)
    return pl.pallas_call(
        functools.partial(_post1_kernel, nlat=rows.nlat, alpha=alpha, ne=ne, ng=ng, sub=min(tm, CFG["tm_post_sub"])),
        grid=(nblk,),
        in_specs=[_row_spec(tm, d)] + rows.two_source_specs(d, ctx_buffers=1) + [
            _const_spec((d, d)),
            rows.mod_spec(d, 2), rows.mod_spec(d, 3), rows.mod_spec(d, 4),
            _const_spec((1, d)), _const_spec((1, d)),
            _const_spec((d, META_W)), _const_spec((1, META_W)),
        ],
        out_specs=[_row_spec(tm, d), _row_spec(tm, pw), _row_spec(tm, META_W),
                   pl.BlockSpec((SUBLANES, META_W), lambda i: (0, 0))],
        out_shape=[
            jax.ShapeDtypeStruct((nrows, d), F32),
            jax.ShapeDtypeStruct((nrows, pw), jnp.uint32),
            jax.ShapeDtypeStruct((nrows, META_W), F32),
            jax.ShapeDtypeStruct((SUBLANES, META_W), F32),
        ],
        scratch_shapes=[pltpu.VMEM((SUBLANES, META_W), F32)],
        compiler_params=_params("arbitrary"),
        name="post_mixer_router",
    )(merged, xa, xb, w_o, m3, m3, m3, ln_g, ln_b, w_r, b_r)


ISSUE_UNROLL = 8


def _dispatch_kernel(dest_ref, tail_ref, t_ref, buf_ref, zero_ref, sem, zsem, *, tm, tme, ne):
    i = pl.program_id(0)

    @pl.when(i == 0)
    def _():
        zero_ref[...] = jnp.zeros(zero_ref.shape, zero_ref.dtype)

        def zero_copy(e):
            row0 = pl.multiple_of(jnp.maximum(tail_ref[e], 0) * tme, tme)
            return pltpu.make_async_copy(zero_ref, buf_ref.at[pl.ds(row0, tme)], zsem)

        for e in range(ne):
            @pl.when(tail_ref[e] >= 0)
            def _():
                zero_copy(e).start()
        for e in range(ne):
            @pl.when(tail_ref[e] >= 0)
            def _():
                zero_copy(e).wait()

    base = i * (2 * tm)
    for r in range(tm):
        for k in range(2):
            pltpu.make_async_copy(t_ref.at[pl.ds(r, 1)], buf_ref.at[pl.ds(dest_ref[base + 2 * r + k], 1)],
                                  sem).start(priority=k)
    for k in range(2):
        pltpu.make_async_copy(t_ref, buf_ref.at[pl.ds(0, tm)], sem).wait()


def _dispatch(dest, tail_blk, t_packed, buf_rows, tm, tme):
    nrows, pw = t_packed.shape
    return pl.pallas_call(
        functools.partial(_dispatch_kernel, tm=tm, tme=tme, ne=tail_blk.shape[0]),
        grid_spec=pltpu.PrefetchScalarGridSpec(
            num_scalar_prefetch=2,
            grid=(nrows // tm,),
            in_specs=[pl.BlockSpec((tm, pw), lambda i, dest, tail: (i, 0))],
            out_specs=pl.BlockSpec(memory_space=pl.ANY),
            scratch_shapes=[pltpu.VMEM((tme, pw), t_packed.dtype),
                            pltpu.SemaphoreType.DMA(()), pltpu.SemaphoreType.DMA(())],
        ),
        out_shape=jax.ShapeDtypeStruct((buf_rows, pw), t_packed.dtype),
        compiler_params=_params("arbitrary"),
        name="moe_dispatch",
    )(dest, tail_blk, t_packed)


def _expert_kernel(be_ref, nu_ref, nxt_ref, x_ref, w1_hbm, w3_hbm, w2_hbm, y_ref,
                   s1_ref, s3_ref, s2_ref, b1_ref, b3_ref, b2_ref, sem, *, layer):
    i = pl.program_id(0)

    def fetch(e):
        return (pltpu.make_async_copy(w1_hbm.at[layer, e], s1_ref, sem.at[0]),
                pltpu.make_async_copy(w3_hbm.at[layer, e], s3_ref, sem.at[1]),
                pltpu.make_async_copy(w2_hbm.at[layer, e], s2_ref, sem.at[2]))

    @pl.when(i < nu_ref[0])
    def _():
        e = be_ref[i]
        run_start = jnp.logical_or(i == 0, e != be_ref[jnp.maximum(i - 1, 0)])

        @pl.when(i == 0)
        def _():
            for cp in fetch(e):
                cp.start()

        @pl.when(run_start)
        def _():
            for cp in fetch(e):
                cp.wait()
            b1_ref[...] = s1_ref[...].astype(BF16)
            b3_ref[...] = s3_ref[...].astype(BF16)
            b2_ref[...] = s2_ref[...].astype(BF16)
            nxt = nxt_ref[i]

            @pl.when(nxt >= 0)
            def _():
                for cp in fetch(nxt):
                    cp.start()

        lo, hi = _unpack_rows(x_ref[...])
        half = lo.shape[1]

        def proj(b_ref):
            return _dot(lo, b_ref[:half, :]) + _dot(hi, b_ref[half:, :])

        a = proj(b1_ref)
        hid = (a * jax.nn.sigmoid(a) * proj(b3_ref)).astype(BF16)
        y_ref[...] = _dot(hid, b2_ref[...])


def _experts(block_e, n_used, next_e, buf, w1, w3, w2, layer):
    tme = CFG["tme"]
    nb = buf.shape[0] // tme
    pw = buf.shape[1]
    _, _, d, eh = w1.shape

    def row_block(i, be, nu, nxt):
        return (jnp.minimum(i, nu[0] - 1), 0)

    hbm = pl.BlockSpec(memory_space=pl.ANY)
    return pl.pallas_call(
        functools.partial(_expert_kernel, layer=layer),
        grid_spec=pltpu.PrefetchScalarGridSpec(
            num_scalar_prefetch=3,
            grid=(nb,),
            in_specs=[pl.BlockSpec((tme, pw), row_block), hbm, hbm, hbm],
            out_specs=pl.BlockSpec((tme, d), row_block),
            scratch_shapes=[pltpu.VMEM((d, eh), F32), pltpu.VMEM((d, eh), F32), pltpu.VMEM((eh, d), F32),
                            pltpu.VMEM((d, eh), BF16), pltpu.VMEM((d, eh), BF16), pltpu.VMEM((eh, d), BF16),
                            pltpu.SemaphoreType.DMA((3,))],
        ),
        out_shape=jax.ShapeDtypeStruct((nb * tme, d), F32),
        compiler_params=_params("arbitrary"),
        name="experts",
    )(block_e, n_used, next_e, buf, w1, w3, w2)


def _combine_kernel(dest_ref, y_ref, meta_ref, x1_ref, gate_ref, lg_ref, lb_ref, sh_ref, sc_ref, *rest,
                    tm, alpha, emit_h, chunk):
    if emit_h:
        x2_ref, h_ref, gbuf, sem = rest
    else:
        x2_ref, gbuf, sem = rest
    i = pl.program_id(0)
    nsteps = pl.num_programs(0)

    def start_row(blk, slot, r):
        a = (blk * tm + r) * 2
        for k in range(2):
            pltpu.make_async_copy(y_ref.at[pl.ds(dest_ref[a + k], 1)], gbuf.at[slot, k, pl.ds(r, 1)],
                                  sem.at[slot]).start(priority=k)

    def wait_slot(slot):
        for k in range(2):
            pltpu.make_async_copy(y_ref.at[pl.ds(0, tm)], gbuf.at[slot, k], sem.at[slot]).wait()

    @pl.when(i == 0)
    def _():
        def body(r, carry):
            start_row(0, 0, r)
            return carry
        lax.fori_loop(0, tm, body, 0, unroll=ISSUE_UNROLL)

    slot = i % 2
    wait_slot(slot)

    nxt = jnp.minimum(i + 1, nsteps - 1)
    for r0 in range(0, tm, chunk):
        for r in range(r0, r0 + chunk):
            start_row(nxt, 1 - slot, r)
        rs = slice(r0, r0 + chunk)
        meta = meta_ref[rs, :]
        f = meta[:, 4:5] * gbuf[slot, 0, rs, :] + meta[:, 5:6] * gbuf[slot, 1, rs, :]
        x2 = _ln(alpha * x1_ref[rs, :] + gate_ref[...] * f) * lg_ref[...] + lb_ref[...]
        x2_ref[rs, :] = x2
        if emit_h:
            h_ref[rs, :] = (_ln(x2) * (1.0 + sc_ref[...]) + sh_ref[...]).astype(BF16)

    @pl.when(i == nsteps - 1)
    def _():
        wait_slot(1 - slot)


def _combine(rows, nblk, dest, ybuf, meta, x1, m3, ln_g, ln_b, m3_next, alpha):
    d = x1.shape[1]
    tm = rows.tm
    nrows = nblk * tm
    emit_h = m3_next is not None
    if not emit_h:
        m3_next = m3

    def rspec(width):
        return pl.BlockSpec((tm, width), lambda i, dest: (i, 0))

    def cspec(shape):
        return pl.BlockSpec(shape, lambda i, dest: (0,) * len(shape))

    def mspec(seg):
        return pl.BlockSpec((None, 1, d), lambda i, dest: (rows.mod_row(i), 0, seg))

    out_specs = [rspec(d)]
    out_shape = [jax.ShapeDtypeStruct((nrows, d), F32)]
    if emit_h:
        out_specs.append(rspec(d))
        out_shape.append(jax.ShapeDtypeStruct((nrows, d), BF16))
    out = pl.pallas_call(
        functools.partial(_combine_kernel, tm=tm, alpha=alpha, emit_h=emit_h, chunk=min(tm, CFG["combine_chunk"])),
        grid_spec=pltpu.PrefetchScalarGridSpec(
            num_scalar_prefetch=1,
            grid=(nblk,),
            in_specs=[pl.BlockSpec(memory_space=pl.ANY), rspec(META_W), rspec(d), mspec(5),
                      cspec((1, d)), cspec((1, d)), mspec(0), mspec(1)],
            out_specs=out_specs,
            scratch_shapes=[pltpu.VMEM((2, 2, tm, d), F32), pltpu.SemaphoreType.DMA((2,))],
        ),
        out_shape=out_shape,
        compiler_params=_params("arbitrary"),
        name="moe_combine",
    )(dest, ybuf, meta, x1, m3, ln_g, ln_b, m3_next, m3_next)
    return out if emit_h else (out[0], None)


def _rot_cols(w):
    q = ROPE_DIM // 4
    return jnp.concatenate([-w[..., q:2 * q], w[..., 0:q], -w[..., 3 * q:4 * q], w[..., 2 * q:3 * q]], axis=-1)


KIND_COPY, KIND_ROPE, KIND_ZERO = 0, 1, 2


def _reorder_kernel(ia_ref, ib_ref, kind_ref, a_ref, b_ref, o_ref):
    del ia_ref, ib_ref
    kind = kind_ref[pl.program_id(0)]
    a = a_ref[...]

    @pl.when(kind == KIND_COPY)
    def _():
        o_ref[...] = jnp.concatenate([a, b_ref[...]], axis=0).T.astype(BF16)

    @pl.when(kind == KIND_ROPE)
    def _():
        q = ROPE_DIM // 4
        rot = jnp.concatenate([-a[q:2 * q], a[0:q], -a[3 * q:4 * q], a[2 * q:3 * q]], axis=0)
        o_ref[...] = jnp.concatenate([a, rot], axis=0).T.astype(BF16)

    @pl.when(kind == KIND_ZERO)
    def _():
        o_ref[...] = jnp.zeros(o_ref.shape, BF16)


def _reorder_w_in(w_in, layer, lay, ql, kvl, gwidth, cwidth):
    _, d, p_in = w_in.shape
    piece = ROPE_DIM
    off_kr = ql + kvl
    off_gm = off_kr + ROPE_DIM
    off_cv = off_gm + 2 * gwidth
    off_gate = off_cv + 3 * cwidth
    src_of = {"qa": 0, "ckv": ql, "gm": off_gm, "cb": off_cv, "cc": off_cv + cwidth, "cx": off_cv + 2 * cwidth,
              "gate": off_gate, "kr": off_kr}
    nblk = lay.width // LANES
    ia = [0] * nblk
    ib = [0] * nblk
    kind = [KIND_ZERO] * nblk
    for name, (off, width, _) in lay.segs.items():
        assert off % LANES == 0 and src_of[name] % piece == 0
        for j in range(width // LANES):
            src = src_of[name] + j * LANES
            ia[off // LANES + j] = src // piece
            ib[off // LANES + j] = src // piece if name == "kr" else src // piece + 1
            kind[off // LANES + j] = KIND_ROPE if name == "kr" else KIND_COPY
    tables = [jnp.asarray(t, dtype=jnp.int32) for t in (ia, ib, kind)]
    return pl.pallas_call(
        _reorder_kernel,
        grid_spec=pltpu.PrefetchScalarGridSpec(
            num_scalar_prefetch=3,
            grid=(nblk,),
            in_specs=[pl.BlockSpec((None, piece, d), lambda j, ia, ib, kind: (layer, ia[j], 0)),
                      pl.BlockSpec((None, piece, d), lambda j, ia, ib, kind: (layer, ib[j], 0))],
            out_specs=pl.BlockSpec((d, LANES), lambda j, ia, ib, kind: (0, j)),
        ),
        out_shape=jax.ShapeDtypeStruct((d, lay.width), BF16),
        compiler_params=_params("arbitrary"),
        name="reorder_w_in",
    )(*tables, *([jnp.swapaxes(w_in, 1, 2)] * 2))


def _rope_table(seq, tm):
    n_rows = seq // GRID_W
    t = jnp.arange(n_rows * GRID_W)
    row = (t // GRID_W).astype(F32)
    col = (t % GRID_W).astype(F32)
    half = ROPE_DIM // 2
    inv = ROPE_BASE ** (-jnp.arange(0, half, 2, dtype=F32) / half)
    ang = jnp.concatenate([row[:, None] * inv] * 2 + [col[:, None] * inv] * 2, axis=-1)
    cs = jnp.concatenate([jnp.cos(ang), jnp.sin(ang)], axis=-1)
    ident = jnp.concatenate([jnp.ones((tm, ROPE_DIM), F32), jnp.zeros((tm, ROPE_DIM), F32)], axis=-1)
    return jnp.concatenate([cs, ident], axis=0)


def kernel(x, c, ctx, c_ctx, w_mod, b_mod, w_in, g_q, w_uq, g_kv, w_ukv, w_oa, gm_ln_g, gm_ln_b, w_s, b_s, w_ob,
           w_conv, w_oc, w_o, ln1_g, ln1_b, w_group, b_group, w_expert, b_expert, w1, w3, w2, ln2_g, ln2_b):
    bsz, seq, d = x.shape
    ctx_len = ctx.shape[1]
    depth = w_mod.shape[0]
    ql = g_q.shape[1]
    kvl = g_kv.shape[1]
    nheads = w_uq.shape[2] // (NOPE_DIM + ROPE_DIM)
    gwidth = gm_ln_g.shape[1]
    cwidth = w_conv.shape[2]
    ne = w_expert.shape[2]
    ng = w_group.shape[2]
    alpha = float((2 * depth) ** 0.25)
    tm, tq, tme = CFG["tm"], CFG["tq"], CFG["tme"]
    rows = _Rows(bsz, seq, ctx_len, tm)
    rows_post = _Rows(bsz, seq, ctx_len, CFG["tm_post"], ctx_blocks_may_span=True)
    lay = _InLayout(d, ql, kvl, gwidth, cwidth, CFG["tn_merge"])
    n_lat, n_ctx = bsz * seq, bsz * ctx_len

    mod_rows = -(-(bsz + 1) // SUBLANES) * SUBLANES
    cc = jnp.zeros((mod_rows, d), F32).at[:bsz].set(c).at[bsz].set(c_ctx)
    m_all = _mod_vectors(cc, w_mod, b_mod)
    cs = _rope_table(seq, tm)

    xa = x.reshape(n_lat, d)
    xb = ctx.reshape(n_ctx, d)
    h = None
    for l in range(depth):
        last = l == depth - 1
        m3 = m_all[l].reshape(mod_rows, 1, 6 * d)
        nblk = rows.nlat if last else rows.nall
        nrows = nblk * tm

        w_all = _reorder_w_in(w_in, l, lay, ql, kvl, gwidth, cwidth)
        uq = w_uq[l].reshape(ql, nheads, NOPE_DIM + ROPE_DIM)
        uq_rope = uq[..., NOPE_DIM:]
        w_uq_l = jnp.concatenate([uq, _rot_cols(uq_rope)], axis=-1).reshape(ql, nheads * HEAD_W).astype(BF16)
        ukv = w_ukv[l].reshape(kvl, nheads, NOPE_DIM + V_DIM)
        w_uk = ukv[..., :NOPE_DIM].reshape(kvl, nheads * NOPE_DIM).astype(BF16)
        w_uv = ukv[..., NOPE_DIM:].reshape(kvl, nheads * V_DIM).astype(BF16)
        bs_exp = jnp.broadcast_to(b_s[l][:, :, None], b_s.shape[1:] + (gwidth // w_s.shape[1],))
        w_r = jnp.zeros((d, META_W), F32).at[:, :ne].set(w_expert[l]).at[:, ne:ne + ng].set(w_group[l]).astype(BF16)
        b_r = jnp.zeros((1, META_W), F32).at[0, :ne].set(b_expert[l]).at[0, ne:ne + ng].set(b_group[l])

        if l == 0:
            h = _ln_modulate(rows, xa, xb, m3)

        q, kcat, vcat, gm, gate_b, y = _mixer_in(rows, nblk, lay, h, w_all, g_q[l][None], w_uq_l, g_kv[l][None],
                                                 w_uk, w_uv, gm_ln_g[l][None], gm_ln_b[l][None],
                                                 w_s[l].astype(BF16), bs_exp, cs, nheads)
        cv = _conv_gate(rows, nblk, gate_b, y, w_conv[l])
        lk = ctx_len + seq
        attn = _attention(q, kcat, vcat, bsz, nheads, seq, min(tq, seq), 0, lk, 1, nrows)
        if not last:
            assert lk % ctx_len == 0
            tqc = min(tq, ctx_len)
            attn = _attention(q, kcat, vcat, bsz, nheads, ctx_len, tqc, n_lat // tqc, ctx_len, lk // ctx_len, nrows,
                              into=attn)
        merged = _merge(nrows, lay, h, attn, gm, cv, w_all, w_oa[l].astype(BF16), w_ob[l].astype(BF16),
                        w_oc[l].astype(BF16))
        x1, t_packed, meta, cnt = _post_mixer(rows_post, nrows // rows_post.tm, merged, xa, xb, w_o[l].astype(BF16),
                                              m3, ln1_g[l][None], ln1_b[l][None], w_r, b_r, alpha, ne, ng)

        e_ids = meta[:, 0:2].astype(jnp.int32)
        ranks = meta[:, 2:4].astype(jnp.int32)
        counts = cnt[0, :ne].astype(jnp.int32)
        experts = jnp.arange(ne, dtype=jnp.int32)
        padded = (counts + tme - 1) // tme * tme
        pend = jnp.cumsum(padded)
        pstart = pend - padded
        dest = (jnp.sum(jnp.where(e_ids[..., None] == experts, pstart, 0), axis=-1) + ranks).reshape(-1)
        nb = 2 * nrows // tme + ne
        n_used = pend[-1:] // tme
        blk = jnp.minimum(jnp.arange(nb, dtype=jnp.int32), n_used[0] - 1)
        block_e = jnp.minimum(jnp.sum(pend[None, :] <= (blk * tme)[:, None], axis=1), ne - 1).astype(jnp.int32)
        tail_blk = jnp.where(padded > 0, pend // tme - 1, -1).astype(jnp.int32)
        nonempty_from = lax.cummin(jnp.where(padded > 0, experts, ne), axis=0, reverse=True)
        next_nonempty = jnp.concatenate([nonempty_from[1:], jnp.full((1,), ne, jnp.int32)])
        next_e = next_nonempty[block_e]
        next_e = jnp.where(next_e >= ne, -1, next_e).astype(jnp.int32)

        buf = _dispatch(dest, tail_blk, t_packed, nb * tme, tm, tme)
        ybuf = _experts(block_e, n_used.astype(jnp.int32), next_e, buf, w1, w3, w2, l)
        m3_next = None if last else m_all[l + 1].reshape(mod_rows, 1, 6 * d)
        x2, h = _combine(rows, nblk, dest, ybuf, meta, x1, m3, ln2_g[l][None], ln2_b[l][None], m3_next, alpha)
        xa, xb = x2, x2
    return xa.reshape(bsz, seq, d)
```

```python
import functools

import jax
import jax.numpy as jnp
from jax import lax
from jax.experimental import pallas as pl
from jax.experimental.pallas import tpu as pltpu

F32 = jnp.float32
BF16 = jnp.bfloat16

GRID_W = 64
NOPE_DIM = 128
ROPE_DIM = 64
V_DIM = 128
ROPE_BASE = 10000.0
LN_EPS = 1e-6
HEAD_W = NOPE_DIM + 2 * ROPE_DIM
META_W = 128
LANES = 128
SUBLANES = 8
LOG2_E = 1.4426950408889634

CFG = dict(
    tm=256,
    tm_post=512,
    tm_post_sub=256,
    tq=2048,
    tq_sub=256,
    tm_merge=512,
    tn_merge=512,
    tn_mod=1024,
    tme=256,
    combine_chunk=32,
)
VMEM_LIMIT = 56 * 1024 * 1024


def _params(*sem):
    return pltpu.CompilerParams(dimension_semantics=sem, vmem_limit_bytes=VMEM_LIMIT)


def _dot(a, b):
    return jnp.dot(a, b, preferred_element_type=F32)


def _ln(x):
    mu = jnp.mean(x, axis=-1, keepdims=True)
    xc = x - mu
    var = jnp.mean(xc * xc, axis=-1, keepdims=True)
    return xc * lax.rsqrt(var + LN_EPS)


def _rms(x):
    return x * lax.rsqrt(jnp.mean(x * x, axis=-1, keepdims=True) + LN_EPS)


def _pack_rows(t):
    half = t.shape[1] // 2
    return pltpu.pack_elementwise([t[:, :half], t[:, half:]], packed_dtype=BF16)


def _unpack_rows(xp):
    lo = pltpu.unpack_elementwise(xp, index=0, packed_dtype=BF16, unpacked_dtype=F32)
    hi = pltpu.unpack_elementwise(xp, index=1, packed_dtype=BF16, unpacked_dtype=F32)
    return lo.astype(BF16), hi.astype(BF16)


def _packed_width(d):
    return d // 2


def _mod_kernel(c_ref, w_ref, b_ref, o_ref):
    c = c_ref[...]
    a = (c * jax.nn.sigmoid(c)).astype(BF16)
    o_ref[...] = _dot(a, w_ref[...].astype(BF16)) + b_ref[...]


def _mod_vectors(cc, w_mod, b_mod):
    nl, d, d6 = w_mod.shape
    rows = cc.shape[0]
    tn = CFG["tn_mod"]
    return pl.pallas_call(
        _mod_kernel,
        grid=(nl, d6 // tn),
        in_specs=[
            pl.BlockSpec((rows, d), lambda l, n: (0, 0)),
            pl.BlockSpec((None, d, tn), lambda l, n: (l, 0, n)),
            pl.BlockSpec((None, 1, tn), lambda l, n: (l, 0, n)),
        ],
        out_specs=pl.BlockSpec((None, rows, tn), lambda l, n: (l, 0, n)),
        out_shape=jax.ShapeDtypeStruct((nl, rows, d6), F32),
        compiler_params=_params("parallel", "parallel"),
        name="mod_vectors",
    )(cc, w_mod, b_mod.reshape(nl, 1, d6))


class _Rows:
    def __init__(self, bsz, seq, ctx_len, tm, ctx_blocks_may_span=False):
        assert seq % tm == 0 and (bsz * ctx_len) % tm == 0
        assert ctx_blocks_may_span or ctx_len % tm == 0
        self.bsz, self.seq, self.ctx_len, self.tm = bsz, seq, ctx_len, tm
        self.nsb = seq // tm
        self.ncb = ctx_len // tm
        self.nlat = bsz * self.nsb
        self.nctx = bsz * ctx_len // tm
        self.nall = self.nlat + self.nctx

    def mod_row(self, i):
        return jnp.where(i < self.nlat, i // self.nsb, self.bsz)

    def mod_spec(self, d, seg):
        return pl.BlockSpec((None, 1, d), lambda i: (self.mod_row(i), 0, seg))

    def rope_spec(self):
        return pl.BlockSpec((self.tm, 2 * ROPE_DIM), lambda i: (jnp.where(i < self.nlat, i % self.nsb, self.nsb), 0))

    def two_source_specs(self, d, ctx_buffers=2):
        return [
            pl.BlockSpec((self.tm, d), lambda i: (jnp.minimum(i, self.nlat - 1), 0)),
            pl.BlockSpec((self.tm, d), lambda i: (jnp.maximum(i - self.nlat, 0), 0),
                         pipeline_mode=pl.Buffered(ctx_buffers)),
        ]

    def seq_edges(self, i):
        j = i - self.nlat
        first = jnp.where(i < self.nlat, i % self.nsb == 0, j % self.ncb == 0)
        last = jnp.where(i < self.nlat, i % self.nsb == self.nsb - 1, j % self.ncb == self.ncb - 1)
        return first, last


def _row_spec(tm, width):
    return pl.BlockSpec((tm, width), lambda i: (i, 0))


def _const_spec(shape):
    return pl.BlockSpec(shape, lambda i: (0,) * len(shape), pipeline_mode=pl.Buffered(1))


def _lnmod_kernel(xa_ref, xb_ref, sh_ref, sc_ref, h_ref, *, nlat):
    x = jnp.where(pl.program_id(0) < nlat, xa_ref[...], xb_ref[...])
    h_ref[...] = (_ln(x) * (1.0 + sc_ref[...]) + sh_ref[...]).astype(BF16)


def _ln_modulate(rows, xa, xb, m3):
    d = xa.shape[1]
    return pl.pallas_call(
        functools.partial(_lnmod_kernel, nlat=rows.nlat),
        grid=(rows.nall,),
        in_specs=rows.two_source_specs(d) + [rows.mod_spec(d, 0), rows.mod_spec(d, 1)],
        out_specs=_row_spec(rows.tm, d),
        out_shape=jax.ShapeDtypeStruct((rows.nall * rows.tm, d), BF16),
        compiler_params=_params("parallel"),
        name="ln_modulate",
    )(xa, xb, m3, m3)


class _InLayout:
    def __init__(self, d, ql, kvl, gwidth, cwidth, tn_gate):
        self.segs = {}
        off = 0
        for name, width, block in (("qa", ql, ql), ("ckv", kvl, kvl), ("gm", 2 * gwidth, 2 * gwidth),
                                   ("cb", cwidth, cwidth), ("cc", cwidth, cwidth), ("cx", cwidth, cwidth),
                                   ("gate", 3 * d, tn_gate), ("kr", 2 * ROPE_DIM, 2 * ROPE_DIM)):
            off = -(-off // block) * block
            self.segs[name] = (off, width, block)
            off += width
        self.width = -(-off // LANES) * LANES

    def block_index(self, name):
        off, _, block = self.segs[name]
        return off // block

    def spec(self, d, name):
        _, width, block = self.segs[name]
        assert width == block
        idx = self.block_index(name)
        return pl.BlockSpec((d, width), lambda i: (0, idx), pipeline_mode=pl.Buffered(1))


def _rope(x2, cs):
    w = x2 * cs
    return w + pltpu.roll(w, ROPE_DIM, axis=1)


def _mixer_in_kernel(h_ref, wqa_ref, wckv_ref, wkr_ref, wgm_ref, wcb_ref, wcc_ref, wcx_ref,
                     gq_ref, wuq_ref, gkv_ref, wuk_ref, wuv_ref, lg_ref, lb_ref, ws_ref, bs_ref, cs_ref,
                     q_ref, k_ref, v_ref, gm_ref, gb_ref, y_ref, *, nheads, nfull, width, chunk, groups):
    h = h_ref[...]
    cs = cs_ref[...]

    c = (_rms(_dot(h, wckv_ref[...])) * gkv_ref[...]).astype(BF16)
    kr = _rope(_dot(h, wkr_ref[...]), cs)
    lane = lax.broadcasted_iota(jnp.int32, kr.shape, 1)
    kr = jnp.where(lane < ROPE_DIM, kr, 0.0).astype(BF16)
    kn = _dot(c, wuk_ref[...]).astype(BF16)
    v_ref[...] = _dot(c, wuv_ref[...]).astype(BF16)
    for hh in range(nheads):
        k_ref[:, hh * HEAD_W:hh * HEAD_W + NOPE_DIM] = kn[:, hh * NOPE_DIM:(hh + 1) * NOPE_DIM]
        k_ref[:, hh * HEAD_W + NOPE_DIM:(hh + 1) * HEAD_W] = kr

    def rest():
        a_q = _dot(h, wqa_ref[...])
        a_gm = _dot(h, wgm_ref[...])
        a_cb = _dot(h, wcb_ref[...])
        a_cc = _dot(h, wcc_ref[...])
        a_cx = _dot(h, wcx_ref[...])
        a = _rms(a_q) * gq_ref[...]
        q = _dot(a.astype(BF16), wuq_ref[...])
        for hh in range(nheads):
            base = hh * HEAD_W
            q_ref[:, base:base + NOPE_DIM] = q[:, base:base + NOPE_DIM].astype(BF16)
            q_ref[:, base + NOPE_DIM:base + HEAD_W] = _rope(q[:, base + NOPE_DIM:base + HEAD_W], cs).astype(BF16)
        z = jax.nn.gelu(a_gm)
        u = z[:, :width]
        vv = (_ln(z[:, width:]) * lg_ref[...] + lb_ref[...]).astype(BF16)
        gw = width // groups
        for ci in range(z.shape[0] // chunk):
            r0 = ci * chunk
            for g in range(groups):
                c0 = g * gw
                mixed = _dot(ws_ref[g], vv[r0:r0 + chunk, c0:c0 + gw]) + bs_ref[g]
                gm_ref[r0:r0 + chunk, c0:c0 + gw] = (u[r0:r0 + chunk, c0:c0 + gw] * mixed).astype(BF16)
        gb_ref[...] = a_cb
        y_ref[...] = a_cc * a_cx

    if nfull is None:
        rest()
    else:
        pl.when(pl.program_id(0) < nfull)(rest)


def _mixer_in(rows, nblk, lay, h, w_all, g_q, w_uq, g_kv, w_uk, w_uv, ln_g, ln_b, w_s, bs_exp, cs, nheads):
    d = h.shape[1]
    tm = rows.tm
    ql = lay.segs["qa"][1]
    kvl = lay.segs["ckv"][1]
    width = lay.segs["gm"][1] // 2
    cw = lay.segs["cb"][1]
    groups, chunk, _ = w_s.shape
    assert tm % chunk == 0
    per_batch = rows.ncb + rows.nsb
    nrows = nblk * tm

    def kv_block(i):
        j = i - rows.nlat
        lat = (i // rows.nsb) * per_batch + rows.ncb + i % rows.nsb
        ctx = (j // rows.ncb) * per_batch + j % rows.ncb
        return jnp.where(i < rows.nlat, lat, ctx)

    def full_spec(w):
        return pl.BlockSpec((tm, w), lambda i: (jnp.minimum(i, nblk - 1), 0))

    def kv_spec(w):
        return pl.BlockSpec((tm, w), lambda i: (kv_block(i), 0))

    return pl.pallas_call(
        functools.partial(_mixer_in_kernel, nheads=nheads, nfull=None if nblk == rows.nall else nblk,
                          width=width, chunk=chunk, groups=groups),
        grid=(rows.nall,),
        in_specs=[_row_spec(tm, d)]
        + [lay.spec(d, name) for name in ("qa", "ckv", "kr", "gm", "cb", "cc", "cx")]
        + [_const_spec((1, ql)), _const_spec((ql, nheads * HEAD_W)),
           _const_spec((1, kvl)), _const_spec((kvl, nheads * NOPE_DIM)), _const_spec((kvl, nheads * V_DIM)),
           _const_spec((1, width)), _const_spec((1, width)),
           _const_spec((groups, chunk, chunk)), _const_spec((groups, chunk, width // groups)),
           rows.rope_spec()],
        out_specs=[full_spec(nheads * HEAD_W), kv_spec(nheads * HEAD_W), kv_spec(nheads * V_DIM),
                   full_spec(width), full_spec(cw), full_spec(cw)],
        out_shape=[
            jax.ShapeDtypeStruct((nrows, nheads * HEAD_W), BF16),
            jax.ShapeDtypeStruct((rows.nall * tm, nheads * HEAD_W), BF16),
            jax.ShapeDtypeStruct((rows.nall * tm, nheads * V_DIM), BF16),
            jax.ShapeDtypeStruct((nrows, width), BF16),
            jax.ShapeDtypeStruct((nrows, cw), F32),
            jax.ShapeDtypeStruct((nrows, cw), F32),
        ],
        compiler_params=_params("arbitrary"),
        name="mixer_in",
    )(h, *([w_all] * 7), g_q, w_uq, g_kv, w_uk, w_uv, ln_g, ln_b, w_s, bs_exp, cs)


def _conv_gate_kernel(gb_ref, y_ref, yp_ref, yn_ref, wc_ref, o_ref, *, rows):
    first, last = rows.seq_edges(pl.program_id(0))
    y = y_ref[...]
    y_prev = jnp.where(first, 0.0, yp_ref[SUBLANES - 1:SUBLANES, :])
    y_next = jnp.where(last, 0.0, yn_ref[0:1, :])
    tm = y.shape[0]
    row = lax.broadcasted_iota(jnp.int32, y.shape, 0)
    y_dn = jnp.where(row == 0, y_prev, pltpu.roll(y, 1, axis=0))
    y_up = jnp.where(row == tm - 1, y_next, pltpu.roll(y, tm - 1, axis=0))
    wc = wc_ref[...]
    conv = y_dn * wc[0:1, :] + y * wc[1:2, :] + y_up * wc[2:3, :]
    o_ref[...] = (gb_ref[...] * conv).astype(BF16)


def _conv_gate(rows, nblk, gate_b, y, w_conv):
    cw = y.shape[1]
    per = rows.tm // SUBLANES
    nhalo = y.shape[0] // SUBLANES
    return pl.pallas_call(
        functools.partial(_conv_gate_kernel, rows=rows),
        grid=(nblk,),
        in_specs=[
            _row_spec(rows.tm, cw),
            _row_spec(rows.tm, cw),
            pl.BlockSpec((SUBLANES, cw), lambda i: (jnp.maximum(i * per - 1, 0), 0)),
            pl.BlockSpec((SUBLANES, cw), lambda i: (jnp.minimum((i + 1) * per, nhalo - 1), 0)),
            _const_spec(w_conv.shape),
        ],
        out_specs=_row_spec(rows.tm, cw),
        out_shape=jax.ShapeDtypeStruct((nblk * rows.tm, cw), BF16),
        compiler_params=_params("parallel"),
        name="conv_gate",
    )(gate_b, y, y, y, w_conv)


def _attn_kernel(q_ref, k_ref, v_ref, *rest, scale, sub):
    o_ref = rest[-1]
    k = k_ref[...]
    v = v_ref[...]
    starts = list(range(0, q_ref.shape[0], sub))

    def scores(r0):
        return lax.dot_general(q_ref[r0:r0 + sub, :], k, (((1,), (1,)), ((), ())), preferred_element_type=F32)

    def weights(s):
        p = jnp.exp2((s - jnp.max(s, axis=-1, keepdims=True)) * (scale * LOG2_E))
        return p.astype(BF16), jnp.sum(p, axis=-1, keepdims=True)

    def emit(r0, p, denom):
        o_ref[r0:r0 + sub, :] = (_dot(p, v) / denom).astype(BF16)

    s_next = scores(starts[0])
    pending = None
    for c, r0 in enumerate(starts):
        s = s_next
        if c + 1 < len(starts):
            s_next = scores(starts[c + 1])
        current = (r0,) + weights(s)
        if pending is not None:
            emit(*pending)
        pending = current
    emit(*pending)


def _attention(q, k, v, bsz, nheads, lq, tq, q_block0, lk, k_block_stride, out_rows, into=None):
    nq = lq // tq
    in_specs = [
        pl.BlockSpec((tq, HEAD_W), lambda b, hh, qi: (q_block0 + b * nq + qi, hh)),
        pl.BlockSpec((lk, HEAD_W), lambda b, hh, qi: (b * k_block_stride, hh)),
        pl.BlockSpec((lk, V_DIM), lambda b, hh, qi: (b * k_block_stride, hh)),
    ]
    args = [q, k, v]
    aliases = {}
    if into is not None:
        in_specs.append(pl.BlockSpec(memory_space=pl.ANY))
        args.append(into)
        aliases = {3: 0}
    return pl.pallas_call(
        functools.partial(_attn_kernel, scale=float(NOPE_DIM + ROPE_DIM) ** -0.5, sub=min(tq, CFG["tq_sub"])),
        grid=(bsz, nheads, nq),
        in_specs=in_specs,
        out_specs=pl.BlockSpec((tq, V_DIM), lambda b, hh, qi: (q_block0 + b * nq + qi, hh)),
        out_shape=jax.ShapeDtypeStruct((out_rows, nheads * V_DIM), BF16),
        input_output_aliases=aliases,
        compiler_params=_params("parallel", "parallel", "parallel"),
        name="attention",
    )(*args)


def _merge_kernel(h_ref, a_ref, gm_ref, cv_ref, wg0_ref, wg1_ref, wg2_ref, woa_ref, wob_ref, woc_ref, o_ref):
    h = h_ref[...]
    m = jax.nn.sigmoid(_dot(h, wg0_ref[...])) * _dot(a_ref[...], woa_ref[...])
    m = m + jax.nn.sigmoid(_dot(h, wg1_ref[...])) * _dot(gm_ref[...], wob_ref[...])
    m = m + jax.nn.sigmoid(_dot(h, wg2_ref[...])) * _dot(cv_ref[...], woc_ref[...])
    o_ref[...] = m.astype(BF16)


def _merge(nrows, lay, h, attn, gm, cv, w_all, w_oa, w_ob, w_oc):
    d = h.shape[1]
    tm, tn = CFG["tm_merge"], CFG["tn_merge"]
    ncol = d // tn
    gate0 = lay.block_index("gate")

    def act(width):
        return pl.BlockSpec((tm, width), lambda n, i: (i, 0))

    def gate_w(k):
        return pl.BlockSpec((d, tn), lambda n, i: (0, gate0 + k * ncol + n))

    def out_w(width):
        return pl.BlockSpec((width, tn), lambda n, i: (0, n))

    return pl.pallas_call(
        _merge_kernel,
        grid=(ncol, nrows // tm),
        in_specs=[act(d), act(attn.shape[1]), act(gm.shape[1]), act(cv.shape[1]),
                  gate_w(0), gate_w(1), gate_w(2),
                  out_w(w_oa.shape[0]), out_w(w_ob.shape[0]), out_w(w_oc.shape[0])],
        out_specs=pl.BlockSpec((tm, tn), lambda n, i: (i, n)),
        out_shape=jax.ShapeDtypeStruct((nrows, d), BF16),
        compiler_params=_params("parallel", "parallel"),
        name="merge_branches",
    )(h, attn, gm, cv, w_all, w_all, w_all, w_oa, w_ob, w_oc)


def _route(logits, carry, ne, ng):
    tm = logits.shape[0]
    neg = -1e30
    lane = lax.broadcasted_iota(jnp.int32, logits.shape, 1).astype(F32)
    big = float(4 * META_W)

    def first_lane(mask):
        return jnp.min(jnp.where(mask, lane, big), axis=-1, keepdims=True)

    gl = jnp.where((lane >= ne) & (lane < ne + ng), logits, neg)
    gmax = jnp.max(gl, axis=-1, keepdims=True)
    g_w = 1.0 / jnp.sum(jnp.exp(gl - gmax), axis=-1, keepdims=True)
    g_idx = first_lane(gl == gmax) - ne
    epg = ne // ng
    in_group = (lane >= g_idx * epg) & (lane < (g_idx + 1) * epg)
    el = jnp.where(in_group, logits, neg)
    ee = jnp.where(in_group, jnp.exp(el - jnp.max(el, axis=-1, keepdims=True)), -1.0)
    v1 = jnp.max(ee, axis=-1, keepdims=True)
    i1 = first_lane(ee == v1)
    ee2 = jnp.where(lane == i1, -1.0, ee)
    v2 = jnp.max(ee2, axis=-1, keepdims=True)
    i2 = first_lane(ee2 == v2)
    w1 = g_w * v1 / (v1 + v2)
    w2 = g_w * v2 / (v1 + v2)

    hit1 = lane == i1
    hit2 = lane == i2
    onehot = jnp.where(hit1 | hit2, 1.0, 0.0)
    r_i = lax.broadcasted_iota(jnp.int32, (tm, tm), 0)
    c_i = lax.broadcasted_iota(jnp.int32, (tm, tm), 1)
    earlier = jnp.where(r_i > c_i, 1.0, 0.0).astype(BF16)
    base = carry + _dot(earlier, onehot.astype(BF16))
    rank1 = jnp.sum(jnp.where(hit1, base, 0.0), axis=-1, keepdims=True)
    rank2 = jnp.sum(jnp.where(hit2, base, 0.0), axis=-1, keepdims=True)
    new_carry = carry + jnp.sum(onehot, axis=0, keepdims=True)

    rec = jnp.zeros(logits.shape, F32)
    for k, val in enumerate((i1, i2, rank1, rank2, w1, w2)):
        rec = jnp.where(lane == float(k), val, rec)
    return rec, new_carry


def _post1_kernel(mg_ref, xa_ref, xb_ref, wo_ref, gate_ref, sh_ref, sc_ref, lg_ref, lb_ref, wr_ref, br_ref,
                  x1_ref, t_ref, meta_ref, cnt_ref, carry_ref, *, nlat, alpha, ne, ng, sub):
    i = pl.program_id(0)

    @pl.when(i == 0)
    def _():
        carry_ref[...] = jnp.zeros(carry_ref.shape, F32)

    carry = carry_ref[0:1, :]
    blocks = [slice(r0, r0 + sub) for r0 in range(0, mg_ref.shape[0], sub)]
    ys = [_dot(mg_ref[rs, :], wo_ref[...]) for rs in blocks]
    for rs, y in zip(blocks, ys):
        x = jnp.where(i < nlat, xa_ref[rs, :], xb_ref[rs, :])
        x1 = _ln(alpha * x + gate_ref[...] * y) * lg_ref[...] + lb_ref[...]
        x1_ref[rs, :] = x1
        t = _ln(x1) * (1.0 + sc_ref[...]) + sh_ref[...]
        t_ref[rs, :] = _pack_rows(t)
        logits = _dot(t.astype(BF16), wr_ref[...]) + br_ref[...]
        rec, carry = _route(logits, carry, ne, ng)
        meta_ref[rs, :] = rec
    carry_ref[...] = jnp.broadcast_to(carry, carry_ref.shape)
    cnt_ref[...] = jnp.broadcast_to(carry, cnt_ref.shape)


def _post_mixer(rows, nblk, merged, xa, xb, w_o, m3, ln_g, ln_b, w_r, b_r, alpha, ne, ng):
    d = merged.shape[1]
    tm = rows.tm
    nrows = nblk * tm
    pw = _packed_width(d)
    return pl.pallas_call(
        functools.partial(_post1_kernel, nlat=rows.nlat, alpha=alpha, ne=ne, ng=ng, sub=min(tm, CFG["tm_post_sub"])),
        grid=(nblk,),
        in_specs=[_row_spec(tm, d)] + rows.two_source_specs(d, ctx_buffers=1) + [
            _const_spec((d, d)),
            rows.mod_spec(d, 2), rows.mod_spec(d, 3), rows.mod_spec(d, 4),
            _const_spec((1, d)), _const_spec((1, d)),
            _const_spec((d, META_W)), _const_spec((1, META_W)),
        ],
        out_specs=[_row_spec(tm, d), _row_spec(tm, pw), _row_spec(tm, META_W),
                   pl.BlockSpec((SUBLANES, META_W), lambda i: (0, 0))],
        out_shape=[
            jax.ShapeDtypeStruct((nrows, d), F32),
            jax.ShapeDtypeStruct((nrows, pw), jnp.uint32),
            jax.ShapeDtypeStruct((nrows, META_W), F32),
            jax.ShapeDtypeStruct((SUBLANES, META_W), F32),
        ],
        scratch_shapes=[pltpu.VMEM((SUBLANES, META_W), F32)],
        compiler_params=_params("arbitrary"),
        name="post_mixer_router",
    )(merged, xa, xb, w_o, m3, m3, m3, ln_g, ln_b, w_r, b_r)


ISSUE_UNROLL = 8


def _dispatch_kernel(dest_ref, tail_ref, t_ref, buf_ref, zero_ref, sem, zsem, *, tm, tme, ne):
    i = pl.program_id(0)

    @pl.when(i == 0)
    def _():
        zero_ref[...] = jnp.zeros(zero_ref.shape, zero_ref.dtype)

        def zero_copy(e):
            row0 = pl.multiple_of(jnp.maximum(tail_ref[e], 0) * tme, tme)
            return pltpu.make_async_copy(zero_ref, buf_ref.at[pl.ds(row0, tme)], zsem)

        for e in range(ne):
            @pl.when(tail_ref[e] >= 0)
            def _():
                zero_copy(e).start()
        for e in range(ne):
            @pl.when(tail_ref[e] >= 0)
            def _():
                zero_copy(e).wait()

    base = i * (2 * tm)
    for r in range(tm):
        for k in range(2):
            pltpu.make_async_copy(t_ref.at[pl.ds(r, 1)], buf_ref.at[pl.ds(dest_ref[base + 2 * r + k], 1)],
                                  sem).start(priority=k)
    for k in range(2):
        pltpu.make_async_copy(t_ref, buf_ref.at[pl.ds(0, tm)], sem).wait()


def _dispatch(dest, tail_blk, t_packed, buf_rows, tm, tme):
    nrows, pw = t_packed.shape
    return pl.pallas_call(
        functools.partial(_dispatch_kernel, tm=tm, tme=tme, ne=tail_blk.shape[0]),
        grid_spec=pltpu.PrefetchScalarGridSpec(
            num_scalar_prefetch=2,
            grid=(nrows // tm,),
            in_specs=[pl.BlockSpec((tm, pw), lambda i, dest, tail: (i, 0))],
            out_specs=pl.BlockSpec(memory_space=pl.ANY),
            scratch_shapes=[pltpu.VMEM((tme, pw), t_packed.dtype),
                            pltpu.SemaphoreType.DMA(()), pltpu.SemaphoreType.DMA(())],
        ),
        out_shape=jax.ShapeDtypeStruct((buf_rows, pw), t_packed.dtype),
        compiler_params=_params("arbitrary"),
        name="moe_dispatch",
    )(dest, tail_blk, t_packed)


def _expert_kernel(be_ref, nu_ref, nxt_ref, x_ref, w1_hbm, w3_hbm, w2_hbm, y_ref,
                   s1_ref, s3_ref, s2_ref, b1_ref, b3_ref, b2_ref, sem, *, layer):
    i = pl.program_id(0)

    def fetch(e):
        return (pltpu.make_async_copy(w1_hbm.at[layer, e], s1_ref, sem.at[0]),
                pltpu.make_async_copy(w3_hbm.at[layer, e], s3_ref, sem.at[1]),
                pltpu.make_async_copy(w2_hbm.at[layer, e], s2_ref, sem.at[2]))

    @pl.when(i < nu_ref[0])
    def _():
        e = be_ref[i]
        run_start = jnp.logical_or(i == 0, e != be_ref[jnp.maximum(i - 1, 0)])

        @pl.when(i == 0)
        def _():
            for cp in fetch(e):
                cp.start()

        @pl.when(run_start)
        def _():
            for cp in fetch(e):
                cp.wait()
            b1_ref[...] = s1_ref[...].astype(BF16)
            b3_ref[...] = s3_ref[...].astype(BF16)
            b2_ref[...] = s2_ref[...].astype(BF16)
            nxt = nxt_ref[i]

            @pl.when(nxt >= 0)
            def _():
                for cp in fetch(nxt):
                    cp.start()

        lo, hi = _unpack_rows(x_ref[...])
        half = lo.shape[1]

        def proj(b_ref):
            return _dot(lo, b_ref[:half, :]) + _dot(hi, b_ref[half:, :])

        a = proj(b1_ref)
        hid = (a * jax.nn.sigmoid(a) * proj(b3_ref)).astype(BF16)
        y_ref[...] = _dot(hid, b2_ref[...])


def _experts(block_e, n_used, next_e, buf, w1, w3, w2, layer):
    tme = CFG["tme"]
    nb = buf.shape[0] // tme
    pw = buf.shape[1]
    _, _, d, eh = w1.shape

    def row_block(i, be, nu, nxt):
        return (jnp.minimum(i, nu[0] - 1), 0)

    hbm = pl.BlockSpec(memory_space=pl.ANY)
    return pl.pallas_call(
        functools.partial(_expert_kernel, layer=layer),
        grid_spec=pltpu.PrefetchScalarGridSpec(
            num_scalar_prefetch=3,
            grid=(nb,),
            in_specs=[pl.BlockSpec((tme, pw), row_block), hbm, hbm, hbm],
            out_specs=pl.BlockSpec((tme, d), row_block),
            scratch_shapes=[pltpu.VMEM((d, eh), F32), pltpu.VMEM((d, eh), F32), pltpu.VMEM((eh, d), F32),
                            pltpu.VMEM((d, eh), BF16), pltpu.VMEM((d, eh), BF16), pltpu.VMEM((eh, d), BF16),
                            pltpu.SemaphoreType.DMA((3,))],
        ),
        out_shape=jax.ShapeDtypeStruct((nb * tme, d), F32),
        compiler_params=_params("arbitrary"),
        name="experts",
    )(block_e, n_used, next_e, buf, w1, w3, w2)


def _combine_kernel(dest_ref, y_ref, meta_ref, x1_ref, gate_ref, lg_ref, lb_ref, sh_ref, sc_ref, *rest,
                    tm, alpha, emit_h, chunk):
    if emit_h:
        x2_ref, h_ref, gbuf, sem = rest
    else:
        x2_ref, gbuf, sem = rest
    i = pl.program_id(0)
    nsteps = pl.num_programs(0)

    def start_row(blk, slot, r):
        a = (blk * tm + r) * 2
        for k in range(2):
            pltpu.make_async_copy(y_ref.at[pl.ds(dest_ref[a + k], 1)], gbuf.at[slot, k, pl.ds(r, 1)],
                                  sem.at[slot]).start(priority=k)

    def wait_slot(slot):
        for k in range(2):
            pltpu.make_async_copy(y_ref.at[pl.ds(0, tm)], gbuf.at[slot, k], sem.at[slot]).wait()

    @pl.when(i == 0)
    def _():
        def body(r, carry):
            start_row(0, 0, r)
            return carry
        lax.fori_loop(0, tm, body, 0, unroll=ISSUE_UNROLL)

    slot = i % 2
    wait_slot(slot)

    nxt = jnp.minimum(i + 1, nsteps - 1)
    for r0 in range(0, tm, chunk):
        for r in range(r0, r0 + chunk):
            start_row(nxt, 1 - slot, r)
        rs = slice(r0, r0 + chunk)
        meta = meta_ref[rs, :]
        f = meta[:, 4:5] * gbuf[slot, 0, rs, :] + meta[:, 5:6] * gbuf[slot, 1, rs, :]
        x2 = _ln(alpha * x1_ref[rs, :] + gate_ref[...] * f) * lg_ref[...] + lb_ref[...]
        x2_ref[rs, :] = x2
        if emit_h:
            h_ref[rs, :] = (_ln(x2) * (1.0 + sc_ref[...]) + sh_ref[...]).astype(BF16)

    @pl.when(i == nsteps - 1)
    def _():
        wait_slot(1 - slot)


def _combine(rows, nblk, dest, ybuf, meta, x1, m3, ln_g, ln_b, m3_next, alpha):
    d = x1.shape[1]
    tm = rows.tm
    nrows = nblk * tm
    emit_h = m3_next is not None
    if not emit_h:
        m3_next = m3

    def rspec(width):
        return pl.BlockSpec((tm, width), lambda i, dest: (i, 0))

    def cspec(shape):
        return pl.BlockSpec(shape, lambda i, dest: (0,) * len(shape))

    def mspec(seg):
        return pl.BlockSpec((None, 1, d), lambda i, dest: (rows.mod_row(i), 0, seg))

    out_specs = [rspec(d)]
    out_shape = [jax.ShapeDtypeStruct((nrows, d), F32)]
    if emit_h:
        out_specs.append(rspec(d))
        out_shape.append(jax.ShapeDtypeStruct((nrows, d), BF16))
    out = pl.pallas_call(
        functools.partial(_combine_kernel, tm=tm, alpha=alpha, emit_h=emit_h, chunk=min(tm, CFG["combine_chunk"])),
        grid_spec=pltpu.PrefetchScalarGridSpec(
            num_scalar_prefetch=1,
            grid=(nblk,),
            in_specs=[pl.BlockSpec(memory_space=pl.ANY), rspec(META_W), rspec(d), mspec(5),
                      cspec((1, d)), cspec((1, d)), mspec(0), mspec(1)],
            out_specs=out_specs,
            scratch_shapes=[pltpu.VMEM((2, 2, tm, d), F32), pltpu.SemaphoreType.DMA((2,))],
        ),
        out_shape=out_shape,
        compiler_params=_params("arbitrary"),
        name="moe_combine",
    )(dest, ybuf, meta, x1, m3, ln_g, ln_b, m3_next, m3_next)
    return out if emit_h else (out[0], None)


def _rot_cols(w):
    q = ROPE_DIM // 4
    return jnp.concatenate([-w[..., q:2 * q], w[..., 0:q], -w[..., 3 * q:4 * q], w[..., 2 * q:3 * q]], axis=-1)


KIND_COPY, KIND_ROPE, KIND_ZERO = 0, 1, 2


def _reorder_kernel(ia_ref, ib_ref, kind_ref, a_ref, b_ref, o_ref):
    del ia_ref, ib_ref
    kind = kind_ref[pl.program_id(0)]
    a = a_ref[...]

    @pl.when(kind == KIND_COPY)
    def _():
        o_ref[...] = jnp.concatenate([a, b_ref[...]], axis=0).T.astype(BF16)

    @pl.when(kind == KIND_ROPE)
    def _():
        q = ROPE_DIM // 4
        rot = jnp.concatenate([-a[q:2 * q], a[0:q], -a[3 * q:4 * q], a[2 * q:3 * q]], axis=0)
        o_ref[...] = jnp.concatenate([a, rot], axis=0).T.astype(BF16)

    @pl.when(kind == KIND_ZERO)
    def _():
        o_ref[...] = jnp.zeros(o_ref.shape, BF16)


def _reorder_w_in(w_in, layer, lay, ql, kvl, gwidth, cwidth):
    _, d, p_in = w_in.shape
    piece = ROPE_DIM
    off_kr = ql + kvl
    off_gm = off_kr + ROPE_DIM
    off_cv = off_gm + 2 * gwidth
    off_gate = off_cv + 3 * cwidth
    src_of = {"qa": 0, "ckv": ql, "gm": off_gm, "cb": off_cv, "cc": off_cv + cwidth, "cx": off_cv + 2 * cwidth,
              "gate": off_gate, "kr": off_kr}
    nblk = lay.width // LANES
    ia = [0] * nblk
    ib = [0] * nblk
    kind = [KIND_ZERO] * nblk
    for name, (off, width, _) in lay.segs.items():
        assert off % LANES == 0 and src_of[name] % piece == 0
        for j in range(width // LANES):
            src = src_of[name] + j * LANES
            ia[off // LANES + j] = src // piece
            ib[off // LANES + j] = src // piece if name == "kr" else src // piece + 1
            kind[off // LANES + j] = KIND_ROPE if name == "kr" else KIND_COPY
    tables = [jnp.asarray(t, dtype=jnp.int32) for t in (ia, ib, kind)]
    return pl.pallas_call(
        _reorder_kernel,
        grid_spec=pltpu.PrefetchScalarGridSpec(
            num_scalar_prefetch=3,
            grid=(nblk,),
            in_specs=[pl.BlockSpec((None, piece, d), lambda j, ia, ib, kind: (layer, ia[j], 0)),
                      pl.BlockSpec((None, piece, d), lambda j, ia, ib, kind: (layer, ib[j], 0))],
            out_specs=pl.BlockSpec((d, LANES), lambda j, ia, ib, kind: (0, j)),
        ),
        out_shape=jax.ShapeDtypeStruct((d, lay.width), BF16),
        compiler_params=_params("arbitrary"),
        name="reorder_w_in",
    )(*tables, *([jnp.swapaxes(w_in, 1, 2)] * 2))


def _rope_table(seq, tm):
    n_rows = seq // GRID_W
    t = jnp.arange(n_rows * GRID_W)
    row = (t // GRID_W).astype(F32)
    col = (t % GRID_W).astype(F32)
    half = ROPE_DIM // 2
    inv = ROPE_BASE ** (-jnp.arange(0, half, 2, dtype=F32) / half)
    ang = jnp.concatenate([row[:, None] * inv] * 2 + [col[:, None] * inv] * 2, axis=-1)
    cs = jnp.concatenate([jnp.cos(ang), jnp.sin(ang)], axis=-1)
    ident = jnp.concatenate([jnp.ones((tm, ROPE_DIM), F32), jnp.zeros((tm, ROPE_DIM), F32)], axis=-1)
    return jnp.concatenate([cs, ident], axis=0)


def kernel(x, c, ctx, c_ctx, w_mod, b_mod, w_in, g_q, w_uq, g_kv, w_ukv, w_oa, gm_ln_g, gm_ln_b, w_s, b_s, w_ob,
           w_conv, w_oc, w_o, ln1_g, ln1_b, w_group, b_group, w_expert, b_expert, w1, w3, w2, ln2_g, ln2_b):
    bsz, seq, d = x.shape
    ctx_len = ctx.shape[1]
    depth = w_mod.shape[0]
    ql = g_q.shape[1]
    kvl = g_kv.shape[1]
    nheads = w_uq.shape[2] // (NOPE_DIM + ROPE_DIM)
    gwidth = gm_ln_g.shape[1]
    cwidth = w_conv.shape[2]
    ne = w_expert.shape[2]
    ng = w_group.shape[2]
    alpha = float((2 * depth) ** 0.25)
    tm, tq, tme = CFG["tm"], CFG["tq"], CFG["tme"]
    rows = _Rows(bsz, seq, ctx_len, tm)
    rows_post = _Rows(bsz, seq, ctx_len, CFG["tm_post"], ctx_blocks_may_span=True)
    lay = _InLayout(d, ql, kvl, gwidth, cwidth, CFG["tn_merge"])
    n_lat, n_ctx = bsz * seq, bsz * ctx_len

    mod_rows = -(-(bsz + 1) // SUBLANES) * SUBLANES
    cc = jnp.zeros((mod_rows, d), F32).at[:bsz].set(c).at[bsz].set(c_ctx)
    m_all = _mod_vectors(cc, w_mod, b_mod)
    cs = _rope_table(seq, tm)

    xa = x.reshape(n_lat, d)
    xb = ctx.reshape(n_ctx, d)
    h = None
    for l in range(depth):
        last = l == depth - 1
        m3 = m_all[l].reshape(mod_rows, 1, 6 * d)
        nblk = rows.nlat if last else rows.nall
        nrows = nblk * tm

        w_all = _reorder_w_in(w_in, l, lay, ql, kvl, gwidth, cwidth)
        uq = w_uq[l].reshape(ql, nheads, NOPE_DIM + ROPE_DIM)
        uq_rope = uq[..., NOPE_DIM:]
        w_uq_l = jnp.concatenate([uq, _rot_cols(uq_rope)], axis=-1).reshape(ql, nheads * HEAD_W).astype(BF16)
        ukv = w_ukv[l].reshape(kvl, nheads, NOPE_DIM + V_DIM)
        w_uk = ukv[..., :NOPE_DIM].reshape(kvl, nheads * NOPE_DIM).astype(BF16)
        w_uv = ukv[..., NOPE_DIM:].reshape(kvl, nheads * V_DIM).astype(BF16)
        bs_exp = jnp.broadcast_to(b_s[l][:, :, None], b_s.shape[1:] + (gwidth // w_s.shape[1],))
        w_r = jnp.zeros((d, META_W), F32).at[:, :ne].set(w_expert[l]).at[:, ne:ne + ng].set(w_group[l]).astype(BF16)
        b_r = jnp.zeros((1, META_W), F32).at[0, :ne].set(b_expert[l]).at[0, ne:ne + ng].set(b_group[l])

        if l == 0:
            h = _ln_modulate(rows, xa, xb, m3)

        q, kcat, vcat, gm, gate_b, y = _mixer_in(rows, nblk, lay, h, w_all, g_q[l][None], w_uq_l, g_kv[l][None],
                                                 w_uk, w_uv, gm_ln_g[l][None], gm_ln_b[l][None],
                                                 w_s[l].astype(BF16), bs_exp, cs, nheads)
        cv = _conv_gate(rows, nblk, gate_b, y, w_conv[l])
        lk = ctx_len + seq
        attn = _attention(q, kcat, vcat, bsz, nheads, seq, min(tq, seq), 0, lk, 1, nrows)
        if not last:
            assert lk % ctx_len == 0
            tqc = min(tq, ctx_len)
            attn = _attention(q, kcat, vcat, bsz, nheads, ctx_len, tqc, n_lat // tqc, ctx_len, lk // ctx_len, nrows,
                              into=attn)
        merged = _merge(nrows, lay, h, attn, gm, cv, w_all, w_oa[l].astype(BF16), w_ob[l].astype(BF16),
                        w_oc[l].astype(BF16))
        x1, t_packed, meta, cnt = _post_mixer(rows_post, nrows // rows_post.tm, merged, xa, xb, w_o[l].astype(BF16),
                                              m3, ln1_g[l][None], ln1_b[l][None], w_r, b_r, alpha, ne, ng)

        e_ids = meta[:, 0:2].astype(jnp.int32)
        ranks = meta[:, 2:4].astype(jnp.int32)
        counts = cnt[0, :ne].astype(jnp.int32)
        experts = jnp.arange(ne, dtype=jnp.int32)
        padded = (counts + tme - 1) // tme * tme
        pend = jnp.cumsum(padded)
        pstart = pend - padded
        dest = (jnp.sum(jnp.where(e_ids[..., None] == experts, pstart, 0), axis=-1) + ranks).reshape(-1)
        nb = 2 * nrows // tme + ne
        n_used = pend[-1:] // tme
        blk = jnp.minimum(jnp.arange(nb, dtype=jnp.int32), n_used[0] - 1)
        block_e = jnp.minimum(jnp.sum(pend[None, :] <= (blk * tme)[:, None], axis=1), ne - 1).astype(jnp.int32)
        tail_blk = jnp.where(padded > 0, pend // tme - 1, -1).astype(jnp.int32)
        nonempty_from = lax.cummin(jnp.where(padded > 0, experts, ne), axis=0, reverse=True)
        next_nonempty = jnp.concatenate([nonempty_from[1:], jnp.full((1,), ne, jnp.int32)])
        next_e = next_nonempty[block_e]
        next_e = jnp.where(next_e >= ne, -1, next_e).astype(jnp.int32)

        buf = _dispatch(dest, tail_blk, t_packed, nb * tme, tm, tme)
        ybuf = _experts(block_e, n_used.astype(jnp.int32), next_e, buf, w1, w3, w2, l)
        m3_next = None if last else m_all[l + 1].reshape(mod_rows, 1, 6 * d)
        x2, h = _combine(rows, nblk, dest, ybuf, meta, x1, m3, ln2_g[l][None], ln2_b[l][None], m3_next, alpha)
        xa, xb = x2, x2
    return xa.reshape(bsz, seq, d)
```
